```python
import math
import jax, jax.numpy as jnp
from jax import lax
import numpy as np

D_MODEL = 4096
BATCH = 1
SEQ = 16384
DEPTH = 2

GRID_W = 64
CTX_LEN = 256
N_MIXERS = 2
N_HEADS = 32
HEAD_DIM = D_MODEL // N_HEADS
WIN_ROWS = 8
WIN_COLS = 16
HYENA_ORDER = 2
SHORT_CONV = 3
FILTER_EMB = 33
FILTER_HIDDEN = 64
DECAY_TARGET = 1e-2
FAST_DECAY_PCT = 0.3
SLOW_DECAY_PCT = 1.5
N_EXPERTS = 64
N_GROUPS = 8
TOPK_GROUPS = 4
TOP_K = 8
EXPERT_HIDDEN = 320
ROUTED_SCALE = 2.5
MOE_BLOCK = 256
NORM_EPS = 1e-6

kernel_name = 'hybrid_natten_hyena_moe_dit'


def rms_norm(x, g):
    xf = x.astype(jnp.float32)
    y = xf * lax.rsqrt(jnp.mean(xf * xf, axis=-1, keepdims=True) + NORM_EPS)
    return y.astype(x.dtype) * g


def modulate(h, shift, scale):
    return h * (1 + scale) + shift


def heads(t):
    return t.reshape(t.shape[:-1] + (N_HEADS, HEAD_DIM))


def neighborhood_attention(h, hc, w_qkv, w_out, q_gain, k_gain, rpb, with_ctx_queries):
    B, S, D = h.shape
    C = hc.shape[1]
    rows = S // GRID_W
    kr = min(WIN_ROWS, rows)
    n_loc = kr * WIN_COLS
    scale = HEAD_DIM ** -0.5
    q, k, v = jnp.split(h @ w_qkv, 3, axis=-1)
    q = rms_norm(heads(q), q_gain) * scale
    k = rms_norm(heads(k), k_gain)
    v = heads(v)
    kc, vc = jnp.split(hc @ w_qkv[:, D:], 2, axis=-1)
    kc = rms_norm(heads(kc), k_gain)
    vc = heads(vc)
    grid = (B, rows, GRID_W, N_HEADS, HEAD_DIM)
    q, k, v = q.reshape(grid), k.reshape(grid), v.reshape(grid)

    row_start = jnp.clip(jnp.arange(rows) - kr // 2, 0, rows - kr)
    col = jnp.arange(GRID_W)
    col_start = jnp.clip(col - WIN_COLS // 2, 0, GRID_W - WIN_COLS)
    col_idx = col_start[:, None] + jnp.arange(WIN_COLS)
    col_rel = col_idx - col[:, None] + (WIN_COLS - 1)
    rpb_c = rpb[:, :, col_rel]

    def row_block(r):
        rs = row_start[r]
        q_r = lax.dynamic_index_in_dim(q, r, axis=1, keepdims=False)
        k_r = lax.dynamic_slice_in_dim(k, rs, kr, axis=1)[:, :, col_idx]
        v_r = lax.dynamic_slice_in_dim(v, rs, kr, axis=1)[:, :, col_idx]
        row_rel = rs + jnp.arange(kr) - r + (WIN_ROWS - 1)
        bias = jnp.swapaxes(rpb_c[:, row_rel], 1, 2)
        s_loc = jnp.einsum('bqhd,brqkhd->bhqrk', q_r, k_r).astype(jnp.float32) + bias
        s_ctx = jnp.einsum('bqhd,bchd->bhqc', q_r, kc).astype(jnp.float32)
        s_all = jnp.concatenate([s_loc.reshape(B, N_HEADS, GRID_W, n_loc), s_ctx], axis=-1)
        p = jax.nn.softmax(s_all, axis=-1).astype(v.dtype)
        p_loc = p[..., :n_loc].reshape(B, N_HEADS, GRID_W, kr, WIN_COLS)
        return (jnp.einsum('bhqrk,brqkhd->bqhd', p_loc, v_r)
                + jnp.einsum('bhqc,bchd->bqhd', p[..., n_loc:], vc))

    o = lax.map(row_block, jnp.arange(rows))
    y = jnp.moveaxis(o, 0, 1).reshape(B, S, D) @ w_out
    if not with_ctx_queries:
        return y, None
    qc = rms_norm(heads(hc @ w_qkv[:, :D]), q_gain) * scale
    pc = jax.nn.softmax(jnp.einsum('bqhd,bkhd->bhqk', qc, kc).astype(jnp.float32), axis=-1).astype(vc.dtype)
    yc = jnp.einsum('bhqk,bkhd->bqhd', pc, vc).reshape(B, C, D) @ w_out
    return y, yc


def short_conv(p, w, b):
    y = lax.conv_general_dilated(p, w[:, None, :], window_strides=(1,),
                                 padding=[(SHORT_CONV // 2, SHORT_CONV // 2)],
                                 dimension_numbers=('NWC', 'WIO', 'NWC'),
                                 feature_group_count=p.shape[-1])
    return y + b


def filter_features(L, w1, b1, w2, b2, w3, b3, freq):
    t = jnp.linspace(0.0, 1.0, L, dtype=jnp.float32)[:, None]
    bands = (FILTER_EMB - 1) // 2
    f = jnp.linspace(1e-4, bands - 1, bands, dtype=jnp.float32)
    ang = (2.0 * math.pi / L) * jnp.arange(L, dtype=jnp.float32)[:, None] * f
    z = jnp.concatenate([t, jnp.cos(ang), -jnp.sin(ang)], axis=-1)
    h = jnp.sin(freq * (z @ w1 + b1))
    h = jnp.sin(freq * (h @ w2 + b2))
    return jnp.sin(freq * (h @ w3 + b3))


def fft_conv(z, kk):
    L = z.shape[1]
    zf = jnp.fft.rfft(z.astype(jnp.float32), n=2 * L, axis=1)
    kf = jnp.fft.rfft(kk, n=2 * L, axis=0)
    return jnp.fft.irfft(zf * kf, n=2 * L, axis=1)[:, :L].astype(z.dtype)


def hyena(u, w_in, b_in, conv_w, conv_b, f_w1, f_b1, f_w2, f_b2, f_w3, f_b3, f_freq, f_wout,
          skip, w_out, b_out):
    B, L, D = u.shape
    parts = jnp.split(short_conv(u @ w_in + b_in, conv_w, conv_b), HYENA_ORDER + 1, axis=-1)
    v, gates = parts[0], parts[1:]
    feats = filter_features(L, f_w1, f_b1, f_w2, f_b2, f_w3, f_b3, f_freq)
    t = jnp.linspace(0.0, 1.0, L, dtype=jnp.float32)[:, None, None]
    deltas = jnp.abs(jnp.linspace(math.log(DECAY_TARGET) / SLOW_DECAY_PCT,
                                  math.log(DECAY_TARGET) / FAST_DECAY_PCT, D, dtype=jnp.float32))
    window = jnp.exp(-t * deltas)
    z = v
    for n in range(HYENA_ORDER):
        h = jnp.einsum('lf,fsd->lsd', feats, f_wout[:, n].astype(jnp.float32)) * window
        kk = jnp.concatenate([h[:, 0], jnp.zeros((1, D), h.dtype), jnp.flip(h[1:, 1], axis=0)], axis=0)
        kk = kk / jnp.sum(jnp.abs(kk), axis=0, keepdims=True)
        z = gates[n] * (fft_conv(z, kk) + skip[n] * z)
    return z @ w_out + b_out


def moe(t, w_router, b_router, w_gate, w_up, w_down, ws_gate, ws_up, ws_down):
    T, D = t.shape
    scores = jax.nn.sigmoid((t @ w_router).astype(jnp.float32))
    choice = scores + b_router.astype(jnp.float32)
    per_group = N_EXPERTS // N_GROUPS
    grp_score = jnp.sum(lax.top_k(choice.reshape(T, N_GROUPS, per_group), 2)[0], axis=-1)
    top_grp = lax.top_k(grp_score, TOPK_GROUPS)[1]
    grp_mask = jnp.any(top_grp[:, :, None] == jnp.arange(N_GROUPS), axis=1)
    choice = jnp.where(jnp.repeat(grp_mask, per_group, axis=1), choice, -jnp.inf)
    top_e = lax.top_k(choice, TOP_K)[1]
    top_s = jnp.take_along_axis(scores, top_e, axis=-1)
    top_w = (top_s / jnp.sum(top_s, axis=-1, keepdims=True) * ROUTED_SCALE).astype(t.dtype)

    n_assign = T * TOP_K
    eid = top_e.reshape(-1)
    tok = jnp.repeat(jnp.arange(T, dtype=jnp.int32), TOP_K)
    wts = top_w.reshape(-1)
    order = jnp.argsort(eid)
    eid_s, tok_s, w_s = eid[order], tok[order], wts[order]
    counts = jnp.bincount(eid, length=N_EXPERTS)
    padded = (counts + MOE_BLOCK - 1) // MOE_BLOCK * MOE_BLOCK
    start = jnp.cumsum(counts) - counts
    pend = jnp.cumsum(padded)
    pstart = pend - padded
    dest = pstart[eid_s] + (jnp.arange(n_assign) - start[eid_s])
    n_blocks = -(-n_assign // MOE_BLOCK) + N_EXPERTS
    n_rows = n_blocks * MOE_BLOCK
    tok_buf = jnp.zeros((n_rows,), jnp.int32).at[dest].set(tok_s)
    w_buf = jnp.zeros((n_rows,), t.dtype).at[dest].set(w_s)
    block_e = jnp.minimum(jnp.searchsorted(pend, jnp.arange(n_blocks) * MOE_BLOCK, side='right'),
                          N_EXPERTS - 1)

    def expert_block(acc, blk):
        e, tb, wb = blk
        xb = t[tb]
        hb = jax.nn.silu(xb @ w_gate[e]) * (xb @ w_up[e])
        return acc.at[tb].add((hb @ w_down[e]) * wb[:, None]), None

    routed, _ = lax.scan(expert_block, jnp.zeros_like(t),
                         (block_e, tok_buf.reshape(n_blocks, MOE_BLOCK), w_buf.reshape(n_blocks, MOE_BLOCK)))
    shared = (jax.nn.silu(t @ ws_gate) * (t @ ws_up)) @ ws_down
    return routed + shared


def setup_inputs(seed: int = 0) -> dict:
    key = jax.random.key(seed)
    keys = iter(jax.random.split(key, 64))

    def nrm(shape, scale):
        return jax.random.normal(next(keys), shape, jnp.float32) * scale

    D = D_MODEL
    n_attn = (DEPTH + N_MIXERS - 1) // N_MIXERS
    n_hyena = DEPTH // N_MIXERS
    F = FILTER_HIDDEN
    return {
        'x': nrm((BATCH, SEQ, D), 1.0),
        'c': nrm((BATCH, D), 1.0),
        'ctx': nrm((BATCH, CTX_LEN, D), 1.0),
        'c_ctx': nrm((D,), 1.0),
        'ada_w': nrm((DEPTH, D, 6 * D), 0.5 * D ** -0.5),
        'ada_b': nrm((DEPTH, 6 * D), 0.01),
        'norm_mix_g': 1.0 + nrm((DEPTH, D), 0.02),
        'norm_ffn_g': 1.0 + nrm((DEPTH, D), 0.02),
        'na_w_qkv': nrm((n_attn, D, 3 * D), D ** -0.5),
        'na_w_out': nrm((n_attn, D, D), D ** -0.5),
        'na_q_gain': 1.0 + nrm((n_attn, HEAD_DIM), 0.02),
        'na_k_gain': 1.0 + nrm((n_attn, HEAD_DIM), 0.02),
        'na_rpb': nrm((n_attn, N_HEADS, 2 * WIN_ROWS - 1, 2 * WIN_COLS - 1), 0.1),
        'hy_w_in': nrm((n_hyena, D, (HYENA_ORDER + 1) * D), D ** -0.5),
        'hy_b_in': nrm((n_hyena, (HYENA_ORDER + 1) * D), 0.01),
        'hy_conv_w': nrm((n_hyena, SHORT_CONV, (HYENA_ORDER + 1) * D), SHORT_CONV ** -0.5),
        'hy_conv_b': nrm((n_hyena, (HYENA_ORDER + 1) * D), 0.01),
        'hy_f_w1': nrm((n_hyena, FILTER_EMB, F), FILTER_EMB ** -0.5),
        'hy_f_b1': nrm((n_hyena, F), 0.1),
        'hy_f_w2': nrm((n_hyena, F, F), F ** -0.5),
        'hy_f_b2': nrm((n_hyena, F), 0.1),
        'hy_f_w3': nrm((n_hyena, F, F), F ** -0.5),
        'hy_f_b3': nrm((n_hyena, F), 0.1),
        'hy_f_freq': 1.0 + nrm((n_hyena, F), 0.02),
        'hy_f_wout': nrm((n_hyena, F, HYENA_ORDER, 2, D), F ** -0.5),
        'hy_skip': nrm((n_hyena, HYENA_ORDER, D), 1.0),
        'hy_w_out': nrm((n_hyena, D, D), D ** -0.5),
        'hy_b_out': nrm((n_hyena, D), 0.01),
        'moe_w_router': nrm((DEPTH, D, N_EXPERTS), D ** -0.5),
        'moe_b_router': nrm((DEPTH, N_EXPERTS), 0.01),
        'moe_w_gate': nrm((DEPTH, N_EXPERTS, D, EXPERT_HIDDEN), D ** -0.5),
        'moe_w_up': nrm((DEPTH, N_EXPERTS, D, EXPERT_HIDDEN), D ** -0.5),
        'moe_w_down': nrm((DEPTH, N_EXPERTS, EXPERT_HIDDEN, D), EXPERT_HIDDEN ** -0.5),
        'moe_ws_gate': nrm((DEPTH, D, EXPERT_HIDDEN), D ** -0.5),
        'moe_ws_up': nrm((DEPTH, D, EXPERT_HIDDEN), D ** -0.5),
        'moe_ws_down': nrm((DEPTH, EXPERT_HIDDEN, D), EXPERT_HIDDEN ** -0.5),
    }


def reference(x, c, ctx, c_ctx, ada_w, ada_b, norm_mix_g, norm_ffn_g,
              na_w_qkv, na_w_out, na_q_gain, na_k_gain, na_rpb,
              hy_w_in, hy_b_in, hy_conv_w, hy_conv_b, hy_f_w1, hy_f_b1, hy_f_w2, hy_f_b2,
              hy_f_w3, hy_f_b3, hy_f_freq, hy_f_wout, hy_skip, hy_w_out, hy_b_out,
              moe_w_router, moe_b_router, moe_w_gate, moe_w_up, moe_w_down,
              moe_ws_gate, moe_ws_up, moe_ws_down):
    B, S, D = x.shape
    sc = jax.nn.silu(c)
    scc = jax.nn.silu(c_ctx)
    hc = ctx
    for i in range(DEPTH):
        j = i // N_MIXERS
        is_attn = i % N_MIXERS == 0
        ctx_later = any(l % N_MIXERS == 0 for l in range(i + 1, DEPTH))
        sh_a, sc_a, g_a, sh_f, sc_f, g_f = jnp.split((sc @ ada_w[i] + ada_b[i])[:, None, :], 6, axis=-1)
        hn = modulate(rms_norm(x, norm_mix_g[i]), sh_a, sc_a)
        if is_attn or ctx_later:
            csh_a, csc_a, cg_a, csh_f, csc_f, cg_f = jnp.split(scc @ ada_w[i] + ada_b[i], 6)
            hcn = modulate(rms_norm(hc, norm_mix_g[i]), csh_a, csc_a)
        if is_attn:
            y, yc = neighborhood_attention(hn, hcn, na_w_qkv[j], na_w_out[j], na_q_gain[j],
                                           na_k_gain[j], na_rpb[j], ctx_later)
        else:
            hy = (hy_w_in[j], hy_b_in[j], hy_conv_w[j], hy_conv_b[j], hy_f_w1[j], hy_f_b1[j],
                  hy_f_w2[j], hy_f_b2[j], hy_f_w3[j], hy_f_b3[j], hy_f_freq[j], hy_f_wout[j],
                  hy_skip[j], hy_w_out[j], hy_b_out[j])
            y = hyena(hn, *hy)
            yc = hyena(hcn, *hy) if ctx_later else None
        x = x + g_a * y
        mp = (moe_w_router[i], moe_b_router[i], moe_w_gate[i], moe_w_up[i], moe_w_down[i],
              moe_ws_gate[i], moe_ws_up[i], moe_ws_down[i])
        hn = modulate(rms_norm(x, norm_ffn_g[i]), sh_f, sc_f).reshape(B * S, D)
        if ctx_later:
            hc = hc + cg_a * yc
            hcn = modulate(rms_norm(hc, norm_ffn_g[i]), csh_f, csc_f).reshape(-1, D)
            y_all = moe(jnp.concatenate([hn, hcn], axis=0), *mp)
            y = y_all[:B * S]
            hc = hc + cg_f * y_all[B * S:].reshape(hc.shape)
        else:
            y = moe(hn, *mp)
        x = x + g_f * y.reshape(B, S, D)
    return x
```

```python
import functools
import math

import numpy as np
import jax
import jax.numpy as jnp
from jax import lax
from jax.experimental import pallas as pl
from jax.experimental.pallas import tpu as pltpu

F32 = jnp.float32
BF16 = jnp.bfloat16

GRID_W = 64
HEAD_DIM = 128
WIN_ROWS = 8
WIN_COLS = 16
SHORT_CONV = 3
HYENA_ORDER = 2
FILTER_EMB = 33
DECAY_TARGET = 1e-2
FAST_DECAY_PCT = 0.3
SLOW_DECAY_PCT = 1.5
N_EXPERTS = 64
N_GROUPS = 8
TOPK_GROUPS = 4
TOP_K = 8
ROUTED_SCALE = 2.5
MOE_BLOCK = 256
NORM_EPS = 1e-6

LANES = 128
VMEM_LIMIT_BYTES = 48 << 20
MASK_VALUE = -1e30

ATT_ROWS = 8
ATT_HALO = 4
FFT_N2 = 256
FFT_CB = 8
NT_DIMS = (((1,), (1,)), ((), ()))


def _cparams(*sem):
    return pltpu.CompilerParams(dimension_semantics=sem, vmem_limit_bytes=VMEM_LIMIT_BYTES)


def _tile(n, t):
    t = min(n, t)
    assert n % t == 0, (n, t)
    return t


def _norm_modulate(x_ref, g_ref, sh_ref, sc_ref, out_ref, chunk=32):
    g = g_ref[...]
    sc1 = 1.0 + sc_ref[...]
    sh = sh_ref[...]
    rows = x_ref.shape[0]
    chunk = min(chunk, rows)

    def body(r, carry):
        sl = pl.ds(pl.multiple_of(r * chunk, chunk), chunk)
        x = x_ref[sl, :]
        ms = jnp.mean(x * x, axis=-1, keepdims=True)
        y = (x * lax.rsqrt(ms + NORM_EPS)) * g
        out_ref[sl, :] = (y * sc1 + sh).astype(out_ref.dtype)
        return carry

    lax.fori_loop(0, rows // chunk, body, 0)


def _qkv_kernel(x_ref, g_ref, sh_ref, sc_ref, w_ref, gain_ref, o_ref, hn_ref, *, n_norm_tiles):
    j = pl.program_id(1)

    @pl.when(j == 0)
    def _():
        _norm_modulate(x_ref, g_ref, sh_ref, sc_ref, hn_ref)

    y = jnp.dot(hn_ref[...], w_ref[...], preferred_element_type=F32)
    heads = [y[:, hh * HEAD_DIM:(hh + 1) * HEAD_DIM] for hh in range(o_ref.shape[0])]

    @pl.when(j < n_norm_tiles)
    def _():
        gain = gain_ref[...]
        for hh, yh in enumerate(heads):
            ms = jnp.mean(yh * yh, axis=-1, keepdims=True)
            o_ref[hh] = ((yh * lax.rsqrt(ms + NORM_EPS)) * gain).astype(o_ref.dtype)

    @pl.when(j >= n_norm_tiles)
    def _():
        for hh, yh in enumerate(heads):
            o_ref[hh] = yh.astype(o_ref.dtype)


def _qkv_proj(x, g, shift, scale, w_bf16, gains, tm=512, tn=512):
    S, D = x.shape
    N = w_bf16.shape[1]
    tm, tn = _tile(S, tm), _tile(D, tn)
    tiles_per_region = D // tn
    return pl.pallas_call(
        functools.partial(_qkv_kernel, n_norm_tiles=2 * tiles_per_region),
        grid=(S // tm, N // tn),
        in_specs=[
            pl.BlockSpec((tm, D), lambda i, j: (i, 0)),
            pl.BlockSpec((1, D), lambda i, j: (0, 0)),
            pl.BlockSpec((1, D), lambda i, j: (0, 0)),
            pl.BlockSpec((1, D), lambda i, j: (0, 0)),
            pl.BlockSpec((D, tn), lambda i, j: (0, j)),
            pl.BlockSpec((None, 1, HEAD_DIM), lambda i, j: (j // tiles_per_region, 0, 0)),
        ],
        out_specs=pl.BlockSpec((tn // HEAD_DIM, tm, HEAD_DIM), lambda i, j: (j, i, 0)),
        out_shape=jax.ShapeDtypeStruct((N // HEAD_DIM, S, HEAD_DIM), BF16),
        scratch_shapes=[pltpu.VMEM((tm, D), BF16)],
        compiler_params=_cparams("parallel", "arbitrary"),
        name="qkv_proj",
    )(x, g, shift, scale, w_bf16, gains)


def _attn_kernel(q_ref, kp_ref, kc_ref, kn_ref, vp_ref, vc_ref, vn_ref, kx_ref, vx_ref,
                 b_ref, o_ref):
    halo = ATT_HALO * GRID_W
    blk = ATT_ROWS * GRID_W
    q = q_ref[...]
    k_parts = (kp_ref[blk - halo:, :], kc_ref[...], kn_ref[:halo, :], kx_ref[...])
    v_parts = (vp_ref[blk - halo:, :], vc_ref[...], vn_ref[:halo, :], vx_ref[...])
    bias_cols = (0, halo, halo + blk, 2 * halo + blk)

    scores = []
    for idx, k in enumerate(k_parts):
        s = lax.dot_general(q, k, NT_DIMS, preferred_element_type=F32)
        if idx < 3:
            s = s + b_ref[:, bias_cols[idx]:bias_cols[idx + 1]]
        scores.append(s)
    m = scores[0].max(axis=-1, keepdims=True)
    for s in scores[1:]:
        m = jnp.maximum(m, s.max(axis=-1, keepdims=True))
    den = jnp.zeros_like(m)
    acc = jnp.zeros(o_ref.shape, F32)
    for s, v in zip(scores, v_parts):
        p = jnp.exp(s - m)
        den = den + p.sum(axis=-1, keepdims=True)
        acc = acc + jnp.dot(p.astype(BF16), v, preferred_element_type=F32)
    o_ref[...] = (acc / den).astype(o_ref.dtype)


def _attention_bias(rpb, rows):
    n_rb = rows // ATT_ROWS
    rb_rep = jnp.array([0, min(1, n_rb - 1), n_rb - 1], jnp.int32)
    rl = jnp.arange(ATT_ROWS)
    i = jnp.arange(ATT_ROWS + 2 * ATT_HALO)
    r = rb_rep[:, None] * ATT_ROWS + rl[None, :]
    a = rb_rep[:, None] * ATT_ROWS - ATT_HALO + i[None, :]
    rs = jnp.clip(r - WIN_ROWS // 2, 0, rows - WIN_ROWS)
    row_ok = (a[:, None, :] >= rs[:, :, None]) & (a[:, None, :] < rs[:, :, None] + WIN_ROWS)
    drow = jnp.clip(a[:, None, :] - r[:, :, None] + (WIN_ROWS - 1), 0, 2 * WIN_ROWS - 2)
    col = jnp.arange(GRID_W)
    cs = jnp.clip(col - WIN_COLS // 2, 0, GRID_W - WIN_COLS)
    col_ok = (col[None, :] >= cs[:, None]) & (col[None, :] < cs[:, None] + WIN_COLS)
    dcol = jnp.clip(col[None, :] - col[:, None] + (WIN_COLS - 1), 0, 2 * WIN_COLS - 2)
    val = rpb[:, drow[:, :, :, None, None], dcol[None, None, None, :, :]]
    ok = row_ok[:, :, :, None, None] & col_ok[None, None, None, :, :]
    val = jnp.where(ok[None], val, MASK_VALUE)
    val = jnp.transpose(val, (1, 0, 2, 4, 3, 5))
    H = rpb.shape[0]
    return val.reshape(3, H, ATT_ROWS * GRID_W, (ATT_ROWS + 2 * ATT_HALO) * GRID_W).astype(F32)


def _attention(qkvh, ctxh, bias, S, D):
    H = D // HEAD_DIM
    C = ctxh.shape[1]
    blk = ATT_ROWS * GRID_W
    n_rb = S // blk
    nkeys = (ATT_ROWS + 2 * ATT_HALO) * GRID_W

    def spec(head_off, shift):
        return pl.BlockSpec(
            (None, blk, HEAD_DIM),
            lambda h, rb: (head_off + h, jnp.clip(rb + shift, 0, n_rb - 1), 0))

    def variant(rb):
        return jnp.where(rb == 0, 0, jnp.where(rb == n_rb - 1, 2, 1))

    return pl.pallas_call(
        _attn_kernel,
        grid=(H, n_rb),
        in_specs=[
            spec(0, 0),
            spec(H, -1), spec(H, 0), spec(H, 1),
            spec(2 * H, -1), spec(2 * H, 0), spec(2 * H, 1),
            pl.BlockSpec((None, C, HEAD_DIM), lambda h, rb: (H + h, 0, 0)),
            pl.BlockSpec((None, C, HEAD_DIM), lambda h, rb: (2 * H + h, 0, 0)),
            pl.BlockSpec((None, None, blk, nkeys), lambda h, rb: (variant(rb), h, 0, 0)),
        ],
        out_specs=pl.BlockSpec((blk, HEAD_DIM), lambda h, rb: (rb, h)),
        out_shape=jax.ShapeDtypeStruct((S, D), BF16),
        compiler_params=_cparams("parallel", "arbitrary"),
        name="nbr_attention",
    )(qkvh, qkvh, qkvh, qkvh, qkvh, qkvh, qkvh, ctxh, ctxh, bias)


def _mm_kernel(*refs, has_bias, has_resid):
    a_ref, w_ref = refs[0], refs[1]
    o_ref = refs[-1]
    y = jnp.dot(a_ref[...], w_ref[...].astype(BF16), preferred_element_type=F32)
    k = 2
    if has_bias:
        y = y + refs[k][...]
        k += 1
    if has_resid:
        y = refs[k][...] + refs[k + 1][...] * y
    o_ref[...] = y.astype(o_ref.dtype)


def _matmul(a, w, bias=None, resid=None, gate=None, out_dtype=F32, tm=512, tn=512, name="matmul"):
    M, K = a.shape
    N = w.shape[1]
    tm, tn = _tile(M, tm), _tile(N, tn)
    in_specs = [pl.BlockSpec((tm, K), lambda i, j: (i, 0)),
                pl.BlockSpec((K, tn), lambda i, j: (0, j))]
    args = [a, w]
    if bias is not None:
        in_specs.append(pl.BlockSpec((1, tn), lambda i, j: (0, j)))
        args.append(bias)
    if resid is not None:
        in_specs += [pl.BlockSpec((tm, tn), lambda i, j: (i, j)),
                     pl.BlockSpec((1, tn), lambda i, j: (0, j))]
        args += [resid, gate]
    return pl.pallas_call(
        functools.partial(_mm_kernel, has_bias=bias is not None, has_resid=resid is not None),
        grid=(M // tm, N // tn),
        in_specs=in_specs,
        out_specs=pl.BlockSpec((tm, tn), lambda i, j: (i, j)),
        out_shape=jax.ShapeDtypeStruct((M, N), out_dtype),
        compiler_params=_cparams("parallel", "arbitrary"),
        name=name,
    )(*args)


def _router_kernel(x_ref, g_ref, sh_ref, sc_ref, w_ref, t_ref, lg_ref, hn_ref):
    _norm_modulate(x_ref, g_ref, sh_ref, sc_ref, hn_ref)
    hn = hn_ref[...]
    t_ref[...] = hn.astype(t_ref.dtype)
    lg_ref[...] = jnp.dot(hn, w_ref[...], preferred_element_type=F32,
                          precision=lax.Precision.HIGHEST)


def _router(x, g, shift, scale, w_router_pad, tm=256):
    T, D = x.shape
    NP = w_router_pad.shape[1]
    tm = _tile(T, tm)
    return pl.pallas_call(
        _router_kernel,
        grid=(T // tm,),
        in_specs=[
            pl.BlockSpec((tm, D), lambda i: (i, 0)),
            pl.BlockSpec((1, D), lambda i: (0, 0)),
            pl.BlockSpec((1, D), lambda i: (0, 0)),
            pl.BlockSpec((1, D), lambda i: (0, 0)),
            pl.BlockSpec((D, NP), lambda i: (0, 0)),
        ],
        out_specs=[pl.BlockSpec((tm, D), lambda i: (i, 0)),
                   pl.BlockSpec((tm, NP), lambda i: (i, 0))],
        out_shape=[jax.ShapeDtypeStruct((T, D), BF16),
                   jax.ShapeDtypeStruct((T, NP), F32)],
        scratch_shapes=[pltpu.VMEM((tm, D), F32)],
        compiler_params=_cparams("parallel"),
        name="moe_router",
    )(x, g, shift, scale, w_router_pad)


def _expert_kernel(be_ref, nused_ref, x_ref, wg_ref, wu_ref, wd_ref, rw_ref, y_ref):
    b = pl.program_id(0)

    @pl.when(b < nused_ref[0])
    def _():
        x = x_ref[...]
        gt = jnp.dot(x, wg_ref[...], preferred_element_type=F32)
        up = jnp.dot(x, wu_ref[...], preferred_element_type=F32)
        h = (gt * jax.nn.sigmoid(gt)) * up
        y = jnp.dot(h.astype(BF16), wd_ref[...], preferred_element_type=F32)
        y_ref[...] = (y * rw_ref[...]).astype(y_ref.dtype)

    @pl.when(b >= nused_ref[0])
    def _():
        y_ref[...] = jnp.zeros(y_ref.shape, y_ref.dtype)


def _experts(xg, wg, wu, wd, row_w, block_e, n_used):
    R, D = xg.shape
    Hd = wg.shape[2]
    n_blocks = R // MOE_BLOCK
    return pl.pallas_call(
        _expert_kernel,
        grid_spec=pltpu.PrefetchScalarGridSpec(
            num_scalar_prefetch=2,
            grid=(n_blocks,),
            in_specs=[
                pl.BlockSpec((MOE_BLOCK, D), lambda b, be, nu: (b, 0)),
                pl.BlockSpec((None, D, Hd), lambda b, be, nu: (be[b], 0, 0)),
                pl.BlockSpec((None, D, Hd), lambda b, be, nu: (be[b], 0, 0)),
                pl.BlockSpec((None, Hd, D), lambda b, be, nu: (be[b], 0, 0)),
                pl.BlockSpec((MOE_BLOCK, 1), lambda b, be, nu: (b, 0)),
            ],
            out_specs=pl.BlockSpec((MOE_BLOCK, D), lambda b, be, nu: (b, 0)),
        ),
        out_shape=jax.ShapeDtypeStruct((R, D), BF16),
        compiler_params=_cparams("arbitrary"),
        name="moe_experts",
    )(block_e, n_used, xg, wg, wu, wd, row_w)


def _combine_kernel(x_ref, ysh_ref, yg_ref, gate_ref, o_ref):
    acc = ysh_ref[...].astype(F32)
    for k in range(yg_ref.shape[0]):
        acc = acc + yg_ref[k].astype(F32)
    o_ref[...] = x_ref[...] + gate_ref[...] * acc


def _combine(x, ysh, yg, gate, tm=128):
    T, D = x.shape
    tm = _tile(T, tm)
    return pl.pallas_call(
        _combine_kernel,
        grid=(T // tm,),
        in_specs=[
            pl.BlockSpec((tm, D), lambda i: (i, 0)),
            pl.BlockSpec((tm, D), lambda i: (i, 0)),
            pl.BlockSpec((yg.shape[0], tm, D), lambda i: (0, i, 0)),
            pl.BlockSpec((1, D), lambda i: (0, 0)),
        ],
        out_specs=pl.BlockSpec((tm, D), lambda i: (i, 0)),
        out_shape=jax.ShapeDtypeStruct((T, D), F32),
        compiler_params=_cparams("parallel"),
        name="moe_combine",
    )(x, ysh, yg, gate)


def _route(logits, b_router):
    T = logits.shape[0]
    scores = jax.nn.sigmoid(logits)
    choice = scores + b_router.astype(F32)
    per_group = N_EXPERTS // N_GROUPS
    grp_score = jnp.sum(lax.top_k(choice.reshape(T, N_GROUPS, per_group), 2)[0], axis=-1)
    top_grp = lax.top_k(grp_score, TOPK_GROUPS)[1]
    grp_mask = jnp.any(top_grp[:, :, None] == jnp.arange(N_GROUPS), axis=1)
    choice = jnp.where(jnp.repeat(grp_mask, per_group, axis=1), choice, -jnp.inf)
    top_e = lax.top_k(choice, TOP_K)[1]
    top_s = jnp.take_along_axis(scores, top_e, axis=-1)
    top_w = top_s / jnp.sum(top_s, axis=-1, keepdims=True) * ROUTED_SCALE
    return top_e, top_w


def _dispatch_plan(top_e, top_w):
    T = top_e.shape[0]
    n_assign = T * TOP_K
    eid = top_e.reshape(-1)
    tok = jnp.repeat(jnp.arange(T, dtype=jnp.int32), TOP_K)
    wts = top_w.reshape(-1)
    order = jnp.argsort(eid)
    eid_s, tok_s, w_s = eid[order], tok[order], wts[order]
    counts = jnp.bincount(eid, length=N_EXPERTS)
    padded = (counts + MOE_BLOCK - 1) // MOE_BLOCK * MOE_BLOCK
    start = jnp.cumsum(counts) - counts
    pend = jnp.cumsum(padded)
    pstart = pend - padded
    dest = (pstart[eid_s] + (jnp.arange(n_assign) - start[eid_s])).astype(jnp.int32)
    n_blocks = -(-n_assign // MOE_BLOCK) + N_EXPERTS
    n_rows = n_blocks * MOE_BLOCK
    tok_buf = jnp.zeros((n_rows,), jnp.int32).at[dest].set(tok_s)
    w_buf = jnp.zeros((n_rows,), F32).at[dest].set(w_s)
    block_e = jnp.minimum(
        jnp.searchsorted(pend, jnp.arange(n_blocks) * MOE_BLOCK, side='right'),
        N_EXPERTS - 1).astype(jnp.int32)
    n_used = (pend[-1] // MOE_BLOCK).astype(jnp.int32).reshape(1)
    pos = jnp.zeros((n_assign,), jnp.int32).at[order].set(dest).reshape(T, TOP_K)
    return tok_buf, w_buf, block_e, n_used, pos


def _moe_layer(x, g, shift, scale, gate, w_router, b_router, w_gate, w_up, w_down,
               ws_gate, ws_up, ws_down):
    T, D = x.shape
    w_router_pad = jnp.zeros((D, LANES), F32).at[:, :N_EXPERTS].set(w_router)
    t_bf16, logits = _router(x, g, shift, scale, w_router_pad)
    top_e, top_w = _route(logits[:, :N_EXPERTS], b_router)
    tok_buf, w_buf, block_e, n_used, pos = _dispatch_plan(top_e, top_w)

    xg = t_bf16[tok_buf]
    y = _experts(xg, w_gate.astype(BF16), w_up.astype(BF16), w_down.astype(BF16),
                 w_buf[:, None], block_e, n_used)
    n_sh = T // MOE_BLOCK
    ysh = _experts(t_bf16, ws_gate.astype(BF16)[None], ws_up.astype(BF16)[None],
                   ws_down.astype(BF16)[None], jnp.ones((T, 1), F32),
                   jnp.zeros((n_sh,), jnp.int32), jnp.full((1,), n_sh, jnp.int32))
    yg = y[pos.T]
    return _combine(x, ysh, yg, gate)


def _inproj_t_kernel(x_ref, g_ref, sh_ref, sc_ref, wt_ref, b_ref, o_ref, hn_ref):
    @pl.when(pl.program_id(1) == 0)
    def _():
        _norm_modulate(x_ref, g_ref, sh_ref, sc_ref, hn_ref)

    y = lax.dot_general(wt_ref[...], hn_ref[...], NT_DIMS, preferred_element_type=F32)
    o_ref[...] = y + b_ref[...]


def _inproj_t(x, g, shift, scale, wt_bf16, bias_col, tm=512, tn=512):
    S, D = x.shape
    N = wt_bf16.shape[0]
    tm, tn = _tile(S, tm), _tile(N, tn)
    return pl.pallas_call(
        _inproj_t_kernel,
        grid=(S // tm, N // tn),
        in_specs=[
            pl.BlockSpec((tm, D), lambda i, j: (i, 0)),
            pl.BlockSpec((1, D), lambda i, j: (0, 0)),
            pl.BlockSpec((1, D), lambda i, j: (0, 0)),
            pl.BlockSpec((1, D), lambda i, j: (0, 0)),
            pl.BlockSpec((tn, D), lambda i, j: (j, 0)),
            pl.BlockSpec((tn, 1), lambda i, j: (j, 0)),
        ],
        out_specs=pl.BlockSpec((tn, tm), lambda i, j: (j, i)),
        out_shape=jax.ShapeDtypeStruct((N, S), F32),
        scratch_shapes=[pltpu.VMEM((tm, D), BF16)],
        compiler_params=_cparams("parallel", "arbitrary"),
        name="hyena_inproj",
    )(x, g, shift, scale, wt_bf16, bias_col)


def _filter_mlp_kernel(z_ref, w1_ref, b1_ref, w2_ref, b2_ref, w3_ref, b3_ref, fq_ref, o_ref):
    hp = lax.Precision.HIGHEST
    fq = fq_ref[...]
    h = jnp.sin(fq * (jnp.dot(z_ref[...], w1_ref[...], precision=hp,
                              preferred_element_type=F32) + b1_ref[...]))
    h = jnp.sin(fq * (jnp.dot(h, w2_ref[...], precision=hp,
                              preferred_element_type=F32) + b2_ref[...]))
    h = jnp.sin(fq * (jnp.dot(h, w3_ref[...], precision=hp,
                              preferred_element_type=F32) + b3_ref[...]))
    o_ref[...] = h.astype(o_ref.dtype)


def _filter_features(L, w1, b1, w2, b2, w3, b3, freq, tl=2048):
    Fh = w1.shape[1]
    t = jnp.linspace(0.0, 1.0, L, dtype=F32)[:, None]
    bands = (FILTER_EMB - 1) // 2
    f = jnp.linspace(1e-4, bands - 1, bands, dtype=F32)
    ang = (2.0 * math.pi / L) * jnp.arange(L, dtype=F32)[:, None] * f
    z = jnp.concatenate([t, jnp.cos(ang), -jnp.sin(ang)], axis=-1)
    zp = jnp.zeros((L, LANES), F32).at[:, :FILTER_EMB].set(z)
    w1p = jnp.zeros((LANES, Fh), F32).at[:FILTER_EMB].set(w1)
    tl = _tile(L, tl)
    row = lambda v: v.reshape(1, -1)
    const = lambda shape: pl.BlockSpec(shape, lambda i: (0, 0))
    feats = pl.pallas_call(
        _filter_mlp_kernel,
        grid=(L // tl,),
        in_specs=[pl.BlockSpec((tl, LANES), lambda i: (i, 0)),
                  const((LANES, Fh)), const((1, Fh)), const((Fh, Fh)), const((1, Fh)),
                  const((Fh, Fh)), const((1, Fh)), const((1, Fh))],
        out_specs=pl.BlockSpec((tl, Fh), lambda i: (i, 0)),
        out_shape=jax.ShapeDtypeStruct((L, Fh), BF16),
        compiler_params=_cparams("parallel"),
        name="hyena_filter_mlp",
    )(zp, w1p, row(b1), w2, row(b2), w3, row(b3), row(freq))
    zero = jnp.zeros((1, Fh), BF16)
    feats_all = jnp.concatenate([feats, zero, feats[:0:-1]], axis=0)
    t_all = jnp.concatenate([t[:, 0], t[:1, 0], t[:0:-1, 0]])[None, :]
    return feats_all, t_all


def _filter_taps_kernel(wt_ref, f_ref, t_ref, d_ref, o_ref):
    h = lax.dot_general(wt_ref[...], f_ref[...], NT_DIMS, preferred_element_type=F32)
    o_ref[...] = (h * jnp.exp(-(d_ref[...] * t_ref[...]))).astype(o_ref.dtype)


def _filter_taps(wout_t, feats_all, t_all, deltas_col, tn=512, tl=2048):
    n_ord, _, D, Fh = wout_t.shape
    N = feats_all.shape[0]
    tn, tl = _tile(D, tn), _tile(N // 2, tl)
    half = (N // 2) // tl
    return pl.pallas_call(
        _filter_taps_kernel,
        grid=(n_ord, D // tn, N // tl),
        in_specs=[
            pl.BlockSpec((None, None, tn, Fh), lambda n, c, l: (n, l // half, c, 0)),
            pl.BlockSpec((tl, Fh), lambda n, c, l: (l, 0)),
            pl.BlockSpec((1, tl), lambda n, c, l: (0, l)),
            pl.BlockSpec((tn, 1), lambda n, c, l: (c, 0)),
        ],
        out_specs=pl.BlockSpec((None, tn, tl), lambda n, c, l: (n, c, l)),
        out_shape=jax.ShapeDtypeStruct((n_ord, D, N), BF16),
        compiler_params=_cparams("parallel", "parallel", "arbitrary"),
        name="hyena_filter_taps",
    )(wout_t, feats_all, t_all, deltas_col)


def _dft_constants(N):
    N2 = FFT_N2
    N1 = N // N2
    n1 = np.arange(N1)
    th1 = 2.0 * np.pi * np.outer(n1, n1) / N1
    fl = np.concatenate([np.cos(th1), -np.sin(th1)], axis=0)
    tht = 2.0 * np.pi * np.outer(n1, np.arange(N2)) / N
    n2 = np.arange(N2)
    th2 = 2.0 * np.pi * np.outer(n2, n2) / N2
    fr, fi = np.cos(th2), -np.sin(th2)
    ilc = np.cos(th1) / N
    ils = -np.sin(th1) / N
    bf = lambda a: jnp.asarray(a, dtype=F32).astype(BF16)
    return dict(
        fl_full=bf(fl), fl_half=bf(fl[:, :N1 // 2]),
        tr=jnp.asarray(np.cos(tht), F32), ti=jnp.asarray(-np.sin(tht), F32),
        fr=bf(fr), fi=bf(fi), nfi=bf(-fi),
        ilc=bf(ilc[:N1 // 2]), ils=bf(ils[:N1 // 2]),
    )


def _dft_forward(z_of, n_ch, n1, fl_ref, tr_ref, ti_ref, fr_ref, fi_ref, nfi_ref, br_ref, bi_ref):
    tr, ti = tr_ref[...], ti_ref[...]
    for c in range(n_ch):
        a = jnp.dot(fl_ref[...], z_of(c), preferred_element_type=F32)
        ar, ai = a[:n1], a[n1:]
        br_ref[c * n1:(c + 1) * n1, :] = (ar * tr - ai * ti).astype(BF16)
        bi_ref[c * n1:(c + 1) * n1, :] = (ar * ti + ai * tr).astype(BF16)
    br, bi = br_ref[...], bi_ref[...]
    xr = (jnp.dot(br, fr_ref[...], preferred_element_type=F32)
          + jnp.dot(bi, nfi_ref[...], preferred_element_type=F32))
    xi = (jnp.dot(br, fi_ref[...], preferred_element_type=F32)
          + jnp.dot(bi, fr_ref[...], preferred_element_type=F32))
    return xr, xi


def _filter_fft_kernel(kk_ref, fl_ref, tr_ref, ti_ref, fr_ref, fi_ref, nfi_ref,
                       kr_ref, ki_ref, br_ref, bi_ref):
    n1 = tr_ref.shape[0]
    n_ch = kk_ref.shape[0] // n1

    def z_of(c):
        kk = kk_ref[c * n1:(c + 1) * n1, :]
        l1 = jnp.sum(jnp.abs(kk.astype(F32)), axis=-1, keepdims=True)
        l1 = jnp.sum(l1, axis=0, keepdims=True)
        return (kk.astype(F32) * (1.0 / l1)).astype(BF16)

    xr, xi = _dft_forward(z_of, n_ch, n1, fl_ref, tr_ref, ti_ref, fr_ref, fi_ref, nfi_ref,
                          br_ref, bi_ref)
    kr_ref[...] = xr.astype(kr_ref.dtype)
    ki_ref[...] = xi.astype(ki_ref.dtype)


def _filter_fft(kk, dc):
    n_ord, D, N = kk.shape
    N1 = N // FFT_N2
    cb = _tile(D, FFT_CB)
    rows = cb * N1
    kk2 = kk.reshape(n_ord, D * N1, FFT_N2)
    const = lambda a: pl.BlockSpec(a.shape, lambda n, c: (0, 0))
    consts = [dc['fl_full'], dc['tr'], dc['ti'], dc['fr'], dc['fi'], dc['nfi']]
    blk = pl.BlockSpec((None, rows, FFT_N2), lambda n, c: (n, c, 0))
    return pl.pallas_call(
        _filter_fft_kernel,
        grid=(n_ord, D // cb),
        in_specs=[blk] + [const(a) for a in consts],
        out_specs=[blk, blk],
        out_shape=[jax.ShapeDtypeStruct((n_ord, D * N1, FFT_N2), BF16)] * 2,
        scratch_shapes=[pltpu.VMEM((rows, FFT_N2), BF16)] * 2,
        compiler_params=_cparams("parallel", "parallel"),
        name="hyena_filter_fft",
    )(kk2, *consts)


def _short_conv(u, p, k, n1h):
    rows, n2 = u.shape
    lane = lax.broadcasted_iota(jnp.int32, u.shape, 1)
    r1 = lax.broadcasted_iota(jnp.int32, u.shape, 0) % n1h
    back = pltpu.roll(u, 1, axis=1)
    prev = jnp.where(lane == 0,
                     jnp.where(r1 == 0, 0.0, pltpu.roll(back, 1, axis=0)), back)
    fwd = pltpu.roll(u, n2 - 1, axis=1)
    nxt = jnp.where(lane == n2 - 1,
                    jnp.where(r1 == n1h - 1, 0.0, pltpu.roll(fwd, rows - 1, axis=0)), fwd)
    c = 4 * k
    return (p[:, c:c + 1] * prev + p[:, c + 1:c + 2] * u + p[:, c + 2:c + 3] * nxt
            + p[:, c + 3:c + 4])


def _hyena_conv_kernel(uv_ref, ug0_ref, ug1_ref, p_ref, kr_ref, ki_ref,
                       fl_ref, tr_ref, ti_ref, fr_ref, fi_ref, nfi_ref, ilc_ref, ils_ref,
                       o_ref, br_ref, bi_ref, y_ref):
    n1 = tr_ref.shape[0]
    n1h = n1 // 2
    n_ch = uv_ref.shape[0] // n1h
    p = p_ref[...]
    tr, ti = tr_ref[...], ti_ref[...]
    gate_refs = (ug0_ref, ug1_ref)

    z = _short_conv(uv_ref[...], p, 0, n1h)
    for n in range(HYENA_ORDER):
        zb = z.astype(BF16)
        xr, xi = _dft_forward(lambda c: zb[c * n1h:(c + 1) * n1h, :], n_ch, n1,
                              fl_ref, tr_ref, ti_ref, fr_ref, fi_ref, nfi_ref, br_ref, bi_ref)
        kr, ki = kr_ref[n].astype(F32), ki_ref[n].astype(F32)
        yr = (xr * kr - xi * ki).astype(BF16)
        yi = (xr * ki + xi * kr).astype(BF16)
        gr = (jnp.dot(yr, fr_ref[...], preferred_element_type=F32)
              + jnp.dot(yi, fi_ref[...], preferred_element_type=F32))
        gi = (jnp.dot(yi, fr_ref[...], preferred_element_type=F32)
              + jnp.dot(yr, nfi_ref[...], preferred_element_type=F32))
        for c in range(n_ch):
            grc, gic = gr[c * n1:(c + 1) * n1], gi[c * n1:(c + 1) * n1]
            hr = (grc * tr + gic * ti).astype(BF16)
            hi = (gic * tr - grc * ti).astype(BF16)
            y_ref[c * n1h:(c + 1) * n1h, :] = (
                jnp.dot(ilc_ref[...], hr, preferred_element_type=F32)
                + jnp.dot(ils_ref[...], hi, preferred_element_type=F32))
        gate = _short_conv(gate_refs[n][...], p, n + 1, n1h)
        z = gate * (y_ref[...] + p[:, 12 + n:13 + n] * z)
    o_ref[...] = z.astype(o_ref.dtype)


def _hyena_conv(ut, params_rows, kr, ki, dc, D, L):
    N1 = 2 * L // FFT_N2
    n1h = N1 // 2
    cb = _tile(D, FFT_CB)
    rows = cb * n1h
    u3 = ut.reshape(3, D * n1h, FFT_N2)
    ublk = lambda part: pl.BlockSpec((None, rows, FFT_N2), lambda c: (part, c, 0))
    const = lambda a: pl.BlockSpec(a.shape, lambda c: (0, 0))
    consts = [dc['fl_half'], dc['tr'], dc['ti'], dc['fr'], dc['fi'], dc['nfi'], dc['ilc'], dc['ils']]
    kblk = pl.BlockSpec((HYENA_ORDER, cb * N1, FFT_N2), lambda c: (0, c, 0))
    out = pl.pallas_call(
        _hyena_conv_kernel,
        grid=(D // cb,),
        in_specs=[ublk(0), ublk(1), ublk(2),
                  pl.BlockSpec((rows, 16), lambda c: (c, 0)), kblk, kblk]
                 + [const(a) for a in consts],
        out_specs=pl.BlockSpec((rows, FFT_N2), lambda c: (c, 0)),
        out_shape=jax.ShapeDtypeStruct((D * n1h, FFT_N2), BF16),
        scratch_shapes=[pltpu.VMEM((cb * N1, FFT_N2), BF16)] * 2
                       + [pltpu.VMEM((rows, FFT_N2), F32)],
        compiler_params=_cparams("parallel"),
        name="hyena_long_conv",
    )(u3, u3, u3, params_rows, kr, ki, *consts)
    return out.reshape(D, L)


def _hyena_layer(x, g, shift, scale, gate, w_in, b_in, conv_w, conv_b, f_w1, f_b1, f_w2, f_b2,
                 f_w3, f_b3, f_freq, f_wout, skip, w_out, b_out):
    L, D = x.shape
    n1h = L // FFT_N2
    ut = _inproj_t(x, g, shift, scale, w_in.T.astype(BF16), b_in[:, None])

    feats_all, t_all = _filter_features(L, f_w1, f_b1, f_w2, f_b2, f_w3, f_b3, f_freq)
    deltas = jnp.abs(jnp.linspace(math.log(DECAY_TARGET) / SLOW_DECAY_PCT,
                                  math.log(DECAY_TARGET) / FAST_DECAY_PCT, D, dtype=F32))
    wout_t = jnp.transpose(f_wout, (1, 2, 3, 0)).astype(BF16)
    kk = _filter_taps(wout_t, feats_all, t_all, deltas[:, None])
    dc = _dft_constants(2 * L)
    kr, ki = _filter_fft(kk, dc)

    cw = conv_w.reshape(SHORT_CONV, HYENA_ORDER + 1, D)
    cb = conv_b.reshape(HYENA_ORDER + 1, D)
    cols = []
    for part in range(HYENA_ORDER + 1):
        cols += [cw[0, part], cw[1, part], cw[2, part], cb[part]]
    cols += [skip[0], skip[1], jnp.zeros((D,), F32), jnp.zeros((D,), F32)]
    params_rows = jnp.repeat(jnp.stack(cols, axis=1), n1h, axis=0)

    zt = _hyena_conv(ut, params_rows, kr, ki, dc, D, L)
    return _matmul(zt.T, w_out.astype(BF16), bias=b_out[None, :], resid=x, gate=gate,
                   name="hyena_outproj")


def _attention_layer(x, hc, g, shift, scale, gate, cshift, cscale, w_qkv, w_out, q_gain, k_gain, rpb):
    S, D = x.shape
    rows = S // GRID_W
    w_bf16 = w_qkv.astype(BF16)
    gains = jnp.stack([q_gain * (HEAD_DIM ** -0.5), k_gain, jnp.ones_like(k_gain)])[:, None, :]
    qkvh = _qkv_proj(x, g, shift, scale, w_bf16, gains)
    ctxh = _qkv_proj(hc, g, cshift, cscale, w_bf16, gains)
    bias = _attention_bias(rpb, rows)
    o = _attention(qkvh, ctxh, bias, S, D)
    return _matmul(o, w_out.astype(BF16), resid=x, gate=gate, name="attn_outproj")


def _ada_rows(sc, scc, ada_w, ada_b):
    D = sc.shape[-1]
    a = jnp.zeros((8, D), F32).at[0].set(sc[0]).at[1].set(scc).astype(BF16)
    return _matmul(a, ada_w, bias=ada_b[None, :], tm=8, tn=512, name="adaln")


def kernel(x, c, ctx, c_ctx, ada_w, ada_b, norm_mix_g, norm_ffn_g, na_w_qkv, na_w_out, na_q_gain, na_k_gain, na_rpb, hy_w_in, hy_b_in, hy_conv_w, hy_conv_b, hy_f_w1, hy_f_b1, hy_f_w2, hy_f_b2, hy_f_w3, hy_f_b3, hy_f_freq, hy_f_wout, hy_skip, hy_w_out, hy_b_out, moe_w_router, moe_b_router, moe_w_gate, moe_w_up, moe_w_down, moe_ws_gate, moe_ws_up, moe_ws_down):
    B, S, D = x.shape
    assert B == 1 and D % HEAD_DIM == 0 and S % (ATT_ROWS * GRID_W) == 0
    depth = ada_w.shape[0]
    sc = jax.nn.silu(c)
    scc = jax.nn.silu(c_ctx)
    xs = x[0]
    hc = ctx[0]
    row = lambda v: v.reshape(1, D)
    for i in range(depth):
        j = i // 2
        ada = _ada_rows(sc, scc, ada_w[i], ada_b[i])
        sh_a, sc_a, g_a, sh_f, sc_f, g_f = [row(v) for v in jnp.split(ada[0], 6)]
        if i % 2 == 0:
            csh_a, csc_a = [row(v) for v in jnp.split(ada[1], 6)[:2]]
            xs = _attention_layer(xs, hc, row(norm_mix_g[i]), sh_a, sc_a, g_a, csh_a, csc_a,
                                  na_w_qkv[j], na_w_out[j], na_q_gain[j], na_k_gain[j],
                                  na_rpb[j])
        else:
            xs = _hyena_layer(xs, row(norm_mix_g[i]), sh_a, sc_a, g_a,
                              hy_w_in[j], hy_b_in[j], hy_conv_w[j], hy_conv_b[j],
                              hy_f_w1[j], hy_f_b1[j], hy_f_w2[j], hy_f_b2[j], hy_f_w3[j], hy_f_b3[j],
                              hy_f_freq[j], hy_f_wout[j], hy_skip[j], hy_w_out[j], hy_b_out[j])
        xs = _moe_layer(xs, row(norm_ffn_g[i]), sh_f, sc_f, g_f,
                        moe_w_router[i], moe_b_router[i], moe_w_gate[i], moe_w_up[i], moe_w_down[i],
                        moe_ws_gate[i], moe_ws_up[i], moe_ws_down[i])
    return xs[None]
```

```python
import functools
import math

import numpy as np
import jax
import jax.numpy as jnp
from jax import lax
from jax.experimental import pallas as pl
from jax.experimental.pallas import tpu as pltpu

F32 = jnp.float32
BF16 = jnp.bfloat16

GRID_W = 64
HEAD_DIM = 128
WIN_ROWS = 8
WIN_COLS = 16
SHORT_CONV = 3
HYENA_ORDER = 2
FILTER_EMB = 33
DECAY_TARGET = 1e-2
FAST_DECAY_PCT = 0.3
SLOW_DECAY_PCT = 1.5
N_EXPERTS = 64
N_GROUPS = 8
TOPK_GROUPS = 4
TOP_K = 8
ROUTED_SCALE = 2.5
MOE_BLOCK = 256
NORM_EPS = 1e-6

LANES = 128
VMEM_LIMIT_BYTES = 48 << 20
MASK_VALUE = -1e30

ATT_ROWS = 8
ATT_HALO = 4
FFT_N2 = 256
FFT_CB = 8
NT_DIMS = (((1,), (1,)), ((), ()))


def _cparams(*sem):
    return pltpu.CompilerParams(dimension_semantics=sem, vmem_limit_bytes=VMEM_LIMIT_BYTES)


def _tile(n, t):
    t = min(n, t)
    assert n % t == 0, (n, t)
    return t


def _norm_modulate(x_ref, g_ref, sh_ref, sc_ref, out_ref, chunk=32):
    g = g_ref[...]
    sc1 = 1.0 + sc_ref[...]
    sh = sh_ref[...]
    rows = x_ref.shape[0]
    chunk = min(chunk, rows)

    def body(r, carry):
        sl = pl.ds(pl.multiple_of(r * chunk, chunk), chunk)
        x = x_ref[sl, :]
        ms = jnp.mean(x * x, axis=-1, keepdims=True)
        y = (x * lax.rsqrt(ms + NORM_EPS)) * g
        out_ref[sl, :] = (y * sc1 + sh).astype(out_ref.dtype)
        return carry

    lax.fori_loop(0, rows // chunk, body, 0)


def _qkv_kernel(x_ref, g_ref, sh_ref, sc_ref, w_ref, gain_ref, o_ref, hn_ref, *, n_norm_tiles):
    j = pl.program_id(1)

    @pl.when(j == 0)
    def _():
        _norm_modulate(x_ref, g_ref, sh_ref, sc_ref, hn_ref)

    y = jnp.dot(hn_ref[...], w_ref[...], preferred_element_type=F32)
    heads = [y[:, hh * HEAD_DIM:(hh + 1) * HEAD_DIM] for hh in range(o_ref.shape[0])]

    @pl.when(j < n_norm_tiles)
    def _():
        gain = gain_ref[...]
        for hh, yh in enumerate(heads):
            ms = jnp.mean(yh * yh, axis=-1, keepdims=True)
            o_ref[hh] = ((yh * lax.rsqrt(ms + NORM_EPS)) * gain).astype(o_ref.dtype)

    @pl.when(j >= n_norm_tiles)
    def _():
        for hh, yh in enumerate(heads):
            o_ref[hh] = yh.astype(o_ref.dtype)


def _qkv_proj(x, g, shift, scale, w_bf16, gains, tm=512, tn=512):
    S, D = x.shape
    N = w_bf16.shape[1]
    tm, tn = _tile(S, tm), _tile(D, tn)
    tiles_per_region = D // tn
    return pl.pallas_call(
        functools.partial(_qkv_kernel, n_norm_tiles=2 * tiles_per_region),
        grid=(S // tm, N // tn),
        in_specs=[
            pl.BlockSpec((tm, D), lambda i, j: (i, 0)),
            pl.BlockSpec((1, D), lambda i, j: (0, 0)),
            pl.BlockSpec((1, D), lambda i, j: (0, 0)),
            pl.BlockSpec((1, D), lambda i, j: (0, 0)),
            pl.BlockSpec((D, tn), lambda i, j: (0, j)),
            pl.BlockSpec((None, 1, HEAD_DIM), lambda i, j: (j // tiles_per_region, 0, 0)),
        ],
        out_specs=pl.BlockSpec((tn // HEAD_DIM, tm, HEAD_DIM), lambda i, j: (j, i, 0)),
        out_shape=jax.ShapeDtypeStruct((N // HEAD_DIM, S, HEAD_DIM), BF16),
        scratch_shapes=[pltpu.VMEM((tm, D), BF16)],
        compiler_params=_cparams("parallel", "arbitrary"),
        name="qkv_proj",
    )(x, g, shift, scale, w_bf16, gains)


def _attn_kernel(q_ref, kp_ref, kc_ref, kn_ref, vp_ref, vc_ref, vn_ref, kx_ref, vx_ref,
                 b_ref, o_ref):
    halo = ATT_HALO * GRID_W
    blk = ATT_ROWS * GRID_W
    q = q_ref[...]
    k_parts = (kp_ref[blk - halo:, :], kc_ref[...], kn_ref[:halo, :], kx_ref[...])
    v_parts = (vp_ref[blk - halo:, :], vc_ref[...], vn_ref[:halo, :], vx_ref[...])
    bias_cols = (0, halo, halo + blk, 2 * halo + blk)

    scores = []
    for idx, k in enumerate(k_parts):
        s = lax.dot_general(q, k, NT_DIMS, preferred_element_type=F32)
        if idx < 3:
            s = s + b_ref[:, bias_cols[idx]:bias_cols[idx + 1]]
        scores.append(s)
    m = scores[0].max(axis=-1, keepdims=True)
    for s in scores[1:]:
        m = jnp.maximum(m, s.max(axis=-1, keepdims=True))
    den = jnp.zeros_like(m)
    acc = jnp.zeros(o_ref.shape, F32)
    for s, v in zip(scores, v_parts):
        p = jnp.exp(s - m)
        den = den + p.sum(axis=-1, keepdims=True)
        acc = acc + jnp.dot(p.astype(BF16), v, preferred_element_type=F32)
    o_ref[...] = (acc / den).astype(o_ref.dtype)


def _attention_bias(rpb, rows):
    n_rb = rows // ATT_ROWS
    rb_rep = np.array([0, min(1, n_rb - 1), n_rb - 1])
    rl = np.arange(ATT_ROWS)
    i = np.arange(ATT_ROWS + 2 * ATT_HALO)
    r = rb_rep[:, None] * ATT_ROWS + rl[None, :]
    a = rb_rep[:, None] * ATT_ROWS - ATT_HALO + i[None, :]
    rs = np.clip(r - WIN_ROWS // 2, 0, rows - WIN_ROWS)
    row_ok = (a[:, None, :] >= rs[:, :, None]) & (a[:, None, :] < rs[:, :, None] + WIN_ROWS)
    drow = np.clip(a[:, None, :] - r[:, :, None] + (WIN_ROWS - 1), 0, 2 * WIN_ROWS - 2)
    col = np.arange(GRID_W)
    cs = np.clip(col - WIN_COLS // 2, 0, GRID_W - WIN_COLS)
    col_ok = (col[None, :] >= cs[:, None]) & (col[None, :] < cs[:, None] + WIN_COLS)
    dcol = np.clip(col[None, :] - col[:, None] + (WIN_COLS - 1), 0, 2 * WIN_COLS - 2)
    tcol = jnp.where(col_ok[None, None], rpb[:, :, dcol], MASK_VALUE)
    val = jnp.take(tcol, drow.reshape(-1), axis=1)
    H = rpb.shape[0]
    val = val.reshape((H,) + drow.shape + (GRID_W, GRID_W))
    val = jnp.where(row_ok[None, :, :, :, None, None], val, MASK_VALUE)
    val = jnp.transpose(val, (1, 0, 2, 4, 3, 5))
    return val.reshape(3, H, ATT_ROWS * GRID_W, (ATT_ROWS + 2 * ATT_HALO) * GRID_W).astype(F32)


def _attention(qkvh, ctxh, bias, S, D):
    H = D // HEAD_DIM
    C = ctxh.shape[1]
    blk = ATT_ROWS * GRID_W
    n_rb = S // blk
    nkeys = (ATT_ROWS + 2 * ATT_HALO) * GRID_W

    def spec(head_off, shift):
        return pl.BlockSpec(
            (None, blk, HEAD_DIM),
            lambda h, rb: (head_off + h, jnp.clip(rb + shift, 0, n_rb - 1), 0))

    def variant(rb):
        return jnp.where(rb == 0, 0, jnp.where(rb == n_rb - 1, 2, 1))

    return pl.pallas_call(
        _attn_kernel,
        grid=(H, n_rb),
        in_specs=[
            spec(0, 0),
            spec(H, -1), spec(H, 0), spec(H, 1),
            spec(2 * H, -1), spec(2 * H, 0), spec(2 * H, 1),
            pl.BlockSpec((None, C, HEAD_DIM), lambda h, rb: (H + h, 0, 0)),
            pl.BlockSpec((None, C, HEAD_DIM), lambda h, rb: (2 * H + h, 0, 0)),
            pl.BlockSpec((None, None, blk, nkeys), lambda h, rb: (variant(rb), h, 0, 0)),
        ],
        out_specs=pl.BlockSpec((blk, HEAD_DIM), lambda h, rb: (rb, h)),
        out_shape=jax.ShapeDtypeStruct((S, D), BF16),
        compiler_params=_cparams("parallel", "arbitrary"),
        name="nbr_attention",
    )(qkvh, qkvh, qkvh, qkvh, qkvh, qkvh, qkvh, ctxh, ctxh, bias)


def _mm_kernel(*refs, has_bias, has_resid):
    a_ref, w_ref = refs[0], refs[1]
    o_ref = refs[-1]
    y = jnp.dot(a_ref[...], w_ref[...].astype(BF16), preferred_element_type=F32)
    k = 2
    if has_bias:
        y = y + refs[k][...]
        k += 1
    if has_resid:
        y = refs[k][...] + refs[k + 1][...] * y
    o_ref[...] = y.astype(o_ref.dtype)


def _matmul(a, w, bias=None, resid=None, gate=None, out_dtype=F32, tm=512, tn=512, name="matmul",
            w_layer=None):
    M, K = a.shape
    N = w.shape[-1]
    tm, tn = _tile(M, tm), _tile(N, tn)
    if w_layer is None:
        w_spec = pl.BlockSpec((K, tn), lambda i, j: (0, j))
    else:
        w_spec = pl.BlockSpec((None, K, tn), lambda i, j: (w_layer, 0, j))
    in_specs = [pl.BlockSpec((tm, K), lambda i, j: (i, 0)), w_spec]
    args = [a, w]
    if bias is not None:
        in_specs.append(pl.BlockSpec((1, tn), lambda i, j: (0, j)))
        args.append(bias)
    if resid is not None:
        in_specs += [pl.BlockSpec((tm, tn), lambda i, j: (i, j)),
                     pl.BlockSpec((1, tn), lambda i, j: (0, j))]
        args += [resid, gate]
    return pl.pallas_call(
        functools.partial(_mm_kernel, has_bias=bias is not None, has_resid=resid is not None),
        grid=(M // tm, N // tn),
        in_specs=in_specs,
        out_specs=pl.BlockSpec((tm, tn), lambda i, j: (i, j)),
        out_shape=jax.ShapeDtypeStruct((M, N), out_dtype),
        compiler_params=_cparams("parallel", "arbitrary"),
        name=name,
    )(*args)


def _router_kernel(x_ref, g_ref, sh_ref, sc_ref, w_ref, t_ref, lg_ref, hn_ref):
    _norm_modulate(x_ref, g_ref, sh_ref, sc_ref, hn_ref)
    hn = hn_ref[...]
    t_ref[...] = hn.astype(t_ref.dtype)
    lg_ref[...] = jnp.dot(hn, w_ref[...], preferred_element_type=F32,
                          precision=lax.Precision.HIGHEST)


def _router(x, g, shift, scale, w_router_pad, tm=256):
    T, D = x.shape
    NP = w_router_pad.shape[1]
    tm = _tile(T, tm)
    return pl.pallas_call(
        _router_kernel,
        grid=(T // tm,),
        in_specs=[
            pl.BlockSpec((tm, D), lambda i: (i, 0)),
            pl.BlockSpec((1, D), lambda i: (0, 0)),
            pl.BlockSpec((1, D), lambda i: (0, 0)),
            pl.BlockSpec((1, D), lambda i: (0, 0)),
            pl.BlockSpec((D, NP), lambda i: (0, 0)),
        ],
        out_specs=[pl.BlockSpec((tm, D), lambda i: (i, 0)),
                   pl.BlockSpec((tm, NP), lambda i: (i, 0))],
        out_shape=[jax.ShapeDtypeStruct((T, D), BF16),
                   jax.ShapeDtypeStruct((T, NP), F32)],
        scratch_shapes=[pltpu.VMEM((tm, D), F32)],
        compiler_params=_cparams("parallel"),
        name="moe_router",
    )(x, g, shift, scale, w_router_pad)


def _expert_kernel(be_ref, nused_ref, x_ref, wg_ref, wu_ref, wd_ref, y_ref):
    b = pl.program_id(0)

    @pl.when(b < nused_ref[0])
    def _():
        x = x_ref[...]
        gt = jnp.dot(x, wg_ref[...].astype(BF16), preferred_element_type=F32)
        up = jnp.dot(x, wu_ref[...].astype(BF16), preferred_element_type=F32)
        h = (gt * jax.nn.sigmoid(gt)) * up
        y = jnp.dot(h.astype(BF16), wd_ref[...].astype(BF16), preferred_element_type=F32)
        y_ref[...] = y.astype(y_ref.dtype)

    @pl.when(b >= nused_ref[0])
    def _():
        y_ref[...] = jnp.zeros(y_ref.shape, y_ref.dtype)


def _experts(xg, wg, wu, wd, layer, block_e, n_used):
    R, D = xg.shape
    Hd = wg.shape[-1]
    n_blocks = R // MOE_BLOCK
    once = pl.Buffered(1)
    return pl.pallas_call(
        _expert_kernel,
        grid_spec=pltpu.PrefetchScalarGridSpec(
            num_scalar_prefetch=2,
            grid=(n_blocks,),
            in_specs=[
                pl.BlockSpec((MOE_BLOCK, D), lambda b, be, nu: (b, 0)),
                pl.BlockSpec((None, None, D, Hd), lambda b, be, nu: (layer, be[b], 0, 0),
                             pipeline_mode=once),
                pl.BlockSpec((None, None, D, Hd), lambda b, be, nu: (layer, be[b], 0, 0),
                             pipeline_mode=once),
                pl.BlockSpec((None, None, Hd, D), lambda b, be, nu: (layer, be[b], 0, 0),
                             pipeline_mode=once),
            ],
            out_specs=pl.BlockSpec((MOE_BLOCK, D), lambda b, be, nu: (b, 0)),
        ),
        out_shape=jax.ShapeDtypeStruct((R, D), BF16),
        compiler_params=_cparams("arbitrary"),
        name="moe_experts",
    )(block_e, n_used, xg, wg, wu, wd)


def _combine_kernel(x_ref, ysh_ref, yg_ref, w_ref, gate_ref, o_ref):
    acc = ysh_ref[...].astype(F32)
    w = w_ref[...]
    for k in range(yg_ref.shape[0]):
        acc = acc + w[:, k:k + 1] * yg_ref[k].astype(F32)
    o_ref[...] = x_ref[...] + gate_ref[...] * acc


def _combine(x, ysh, yg, top_w, gate, tm=128):
    T, D = x.shape
    tm = _tile(T, tm)
    return pl.pallas_call(
        _combine_kernel,
        grid=(T // tm,),
        in_specs=[
            pl.BlockSpec((tm, D), lambda i: (i, 0)),
            pl.BlockSpec((tm, D), lambda i: (i, 0)),
            pl.BlockSpec((yg.shape[0], tm, D), lambda i: (0, i, 0)),
            pl.BlockSpec((tm, top_w.shape[1]), lambda i: (i, 0)),
            pl.BlockSpec((1, D), lambda i: (0, 0)),
        ],
        out_specs=pl.BlockSpec((tm, D), lambda i: (i, 0)),
        out_shape=jax.ShapeDtypeStruct((T, D), F32),
        compiler_params=_cparams("parallel"),
        name="moe_combine",
    )(x, ysh, yg, top_w, gate)


def _route(logits, b_router):
    T = logits.shape[0]
    scores = jax.nn.sigmoid(logits)
    choice = scores + b_router.astype(F32)
    per_group = N_EXPERTS // N_GROUPS
    grp_score = jnp.sum(lax.top_k(choice.reshape(T, N_GROUPS, per_group), 2)[0], axis=-1)
    top_grp = lax.top_k(grp_score, TOPK_GROUPS)[1]
    grp_mask = jnp.any(top_grp[:, :, None] == jnp.arange(N_GROUPS), axis=1)
    choice = jnp.where(jnp.repeat(grp_mask, per_group, axis=1), choice, -jnp.inf)
    top_e = lax.top_k(choice, TOP_K)[1]
    top_s = jnp.take_along_axis(scores, top_e, axis=-1)
    top_w = top_s / jnp.sum(top_s, axis=-1, keepdims=True) * ROUTED_SCALE
    return top_e, top_w


def _slot_kernel(eid_ref, pstart_ref, dest_ref, next_ref):
    n, _, ch = eid_ref.shape
    n_exp = pstart_ref.shape[0]
    next_ref[...] = pstart_ref[...]
    e_iota = lax.broadcasted_iota(jnp.int32, (n_exp, ch), 0)
    earlier = (lax.broadcasted_iota(jnp.int32, (ch, ch), 0)
               < lax.broadcasted_iota(jnp.int32, (ch, ch), 1)).astype(BF16)

    def body(r, carry):
        onehot = (eid_ref[r] == e_iota).astype(F32)
        before = jnp.dot(onehot.astype(BF16), earlier, preferred_element_type=F32)
        slot = jnp.sum(onehot * (next_ref[...] + before), axis=0, keepdims=True)
        dest_ref[r] = slot.astype(jnp.int32)
        next_ref[...] = next_ref[...] + jnp.sum(onehot, axis=1, keepdims=True)
        return carry

    lax.fori_loop(0, n, body, 0)


def _dispatch_plan(top_e, ch=256):
    T = top_e.shape[0]
    n_assign = T * TOP_K
    eid = top_e.reshape(-1).astype(jnp.int32)
    counts = jnp.sum((eid[:, None] == jnp.arange(N_EXPERTS)).astype(jnp.int32), axis=0)
    padded = (counts + MOE_BLOCK - 1) // MOE_BLOCK * MOE_BLOCK
    pend = jnp.cumsum(padded)
    pstart = pend - padded
    n_blocks = -(-n_assign // MOE_BLOCK) + N_EXPERTS
    n_rows = n_blocks * MOE_BLOCK
    ch = _tile(n_assign, ch)
    dest = pl.pallas_call(
        _slot_kernel,
        out_shape=jax.ShapeDtypeStruct((n_assign // ch, 1, ch), jnp.int32),
        scratch_shapes=[pltpu.VMEM((N_EXPERTS, 1), F32)],
        compiler_params=pltpu.CompilerParams(vmem_limit_bytes=VMEM_LIMIT_BYTES),
        name="moe_slots",
    )(eid.reshape(n_assign // ch, 1, ch), pstart.astype(F32)[:, None])
    dest = dest.reshape(-1)
    tok = jnp.repeat(jnp.arange(T, dtype=jnp.int32), TOP_K)
    tok_buf = jnp.zeros((n_rows,), jnp.int32).at[dest].set(tok)
    block_start = jnp.arange(n_blocks, dtype=jnp.int32) * MOE_BLOCK
    block_e = jnp.minimum(jnp.sum((pend[None, :] <= block_start[:, None]).astype(jnp.int32), axis=1),
                          N_EXPERTS - 1).astype(jnp.int32)
    n_used = (pend[-1] // MOE_BLOCK).astype(jnp.int32).reshape(1)
    return dest.reshape(T, TOP_K), tok_buf, block_e, n_used


def _moe_layer(x, g, shift, scale, gate, layer, w_router, b_router, w_gate, w_up, w_down,
               ws_gate, ws_up, ws_down):
    T, D = x.shape
    w_router_pad = jnp.zeros((D, LANES), F32).at[:, :N_EXPERTS].set(w_router)
    t_bf16, logits = _router(x, g, shift, scale, w_router_pad)
    top_e, top_w = _route(logits[:, :N_EXPERTS], b_router)
    pos, tok_buf, block_e, n_used = _dispatch_plan(top_e)

    xg = t_bf16[tok_buf]
    y = _experts(xg, w_gate, w_up, w_down, layer, block_e, n_used)
    n_sh = T // MOE_BLOCK
    ysh = _experts(t_bf16, ws_gate[:, None], ws_up[:, None], ws_down[:, None], layer,
                   jnp.zeros((n_sh,), jnp.int32), jnp.full((1,), n_sh, jnp.int32))
    yg = y[pos.T]
    return _combine(x, ysh, yg, top_w, gate)


def _norm_mod_kernel(x_ref, g_ref, sh_ref, sc_ref, o_ref):
    _norm_modulate(x_ref, g_ref, sh_ref, sc_ref, o_ref)


def _norm_mod(x, g, shift, scale, tm=256):
    S, D = x.shape
    tm = _tile(S, tm)
    vec = pl.BlockSpec((1, D), lambda i: (0, 0))
    return pl.pallas_call(
        _norm_mod_kernel,
        grid=(S // tm,),
        in_specs=[pl.BlockSpec((tm, D), lambda i: (i, 0)), vec, vec, vec],
        out_specs=pl.BlockSpec((tm, D), lambda i: (i, 0)),
        out_shape=jax.ShapeDtypeStruct((S, D), BF16),
        compiler_params=_cparams("parallel"),
        name="norm_modulate",
    )(x, g, shift, scale)


def _inproj_t_kernel(wt_ref, hn_ref, b_ref, o_ref, *, groups):
    y = lax.dot_general(wt_ref[...], hn_ref[...], NT_DIMS, preferred_element_type=F32) + b_ref[...]
    per = y.shape[1] // FFT_N2
    i = pl.program_id(1)
    for s in range(groups):
        @pl.when(i % groups == s)
        def _():
            for q in range(per):
                o_ref[:, s * per + q, :] = y[:, q * FFT_N2:(q + 1) * FFT_N2]


def _inproj_t(hn, wt_bf16, bias_col, tm=512, tn=512):
    S, D = hn.shape
    N = wt_bf16.shape[0]
    tm, tn = _tile(S, tm), _tile(N, tn)
    n_chunks = S // FFT_N2
    per = tm // FFT_N2
    rows = min(8, n_chunks)
    groups = rows // per
    return pl.pallas_call(
        functools.partial(_inproj_t_kernel, groups=groups),
        grid=(N // tn, S // tm),
        in_specs=[
            pl.BlockSpec((tn, D), lambda j, i: (j, 0)),
            pl.BlockSpec((tm, D), lambda j, i: (i, 0)),
            pl.BlockSpec((tn, 1), lambda j, i: (j, 0)),
        ],
        out_specs=pl.BlockSpec((tn, rows, FFT_N2), lambda j, i: (j, i // groups, 0)),
        out_shape=jax.ShapeDtypeStruct((N, n_chunks, FFT_N2), F32),
        compiler_params=_cparams("parallel", "arbitrary"),
        name="hyena_inproj",
    )(wt_bf16, hn, bias_col)


def _filter_mlp_kernel(z_ref, w1_ref, b1_ref, w2_ref, b2_ref, w3_ref, b3_ref, fq_ref, o_ref):
    hp = lax.Precision.HIGHEST
    fq = fq_ref[...]
    h = jnp.sin(fq * (jnp.dot(z_ref[...], w1_ref[...], precision=hp,
                              preferred_element_type=F32) + b1_ref[...]))
    h = jnp.sin(fq * (jnp.dot(h, w2_ref[...], precision=hp,
                              preferred_element_type=F32) + b2_ref[...]))
    h = jnp.sin(fq * (jnp.dot(h, w3_ref[...], precision=hp,
                              preferred_element_type=F32) + b3_ref[...]))
    o_ref[...] = h.astype(o_ref.dtype)


def _filter_features(L, w1, b1, w2, b2, w3, b3, freq, tl=2048):
    Fh = w1.shape[1]
    t = jnp.linspace(0.0, 1.0, L, dtype=F32)[:, None]
    bands = (FILTER_EMB - 1) // 2
    f = jnp.linspace(1e-4, bands - 1, bands, dtype=F32)
    ang = (2.0 * math.pi / L) * jnp.arange(L, dtype=F32)[:, None] * f
    z = jnp.concatenate([t, jnp.cos(ang), -jnp.sin(ang)], axis=-1)
    zp = jnp.zeros((L, LANES), F32).at[:, :FILTER_EMB].set(z)
    w1p = jnp.zeros((LANES, Fh), F32).at[:FILTER_EMB].set(w1)
    tl = _tile(L, tl)
    row = lambda v: v.reshape(1, -1)
    const = lambda shape: pl.BlockSpec(shape, lambda i: (0, 0))
    feats = pl.pallas_call(
        _filter_mlp_kernel,
        grid=(L // tl,),
        in_specs=[pl.BlockSpec((tl, LANES), lambda i: (i, 0)),
                  const((LANES, Fh)), const((1, Fh)), const((Fh, Fh)), const((1, Fh)),
                  const((Fh, Fh)), const((1, Fh)), const((1, Fh))],
        out_specs=pl.BlockSpec((tl, Fh), lambda i: (i, 0)),
        out_shape=jax.ShapeDtypeStruct((L, Fh), BF16),
        compiler_params=_cparams("parallel"),
        name="hyena_filter_mlp",
    )(zp, w1p, row(b1), w2, row(b2), w3, row(b3), row(freq))
    zero = jnp.zeros((1, Fh), BF16)
    feats_all = jnp.concatenate([feats, zero, feats[:0:-1]], axis=0)
    t_all = jnp.concatenate([t[:, 0], t[:1, 0], t[:0:-1, 0]])[None, :]
    return feats_all, t_all


def _filter_taps_kernel(wt_ref, f_ref, t_ref, d_ref, o_ref):
    h = lax.dot_general(wt_ref[...], f_ref[...], NT_DIMS, preferred_element_type=F32)
    h = (h * jnp.exp(-(d_ref[...] * t_ref[...]))).astype(o_ref.dtype)
    for q in range(o_ref.shape[1]):
        o_ref[:, q, :] = h[:, q * FFT_N2:(q + 1) * FFT_N2]


def _filter_taps(wout_t, feats_all, t_all, deltas_col, tn=512):
    n_ord, _, D, Fh = wout_t.shape
    N = feats_all.shape[0]
    rows = 8
    tl = rows * FFT_N2
    assert (N // 2) % tl == 0, (N, tl)
    tn = _tile(D, tn)
    half = (N // 2) // tl
    return pl.pallas_call(
        _filter_taps_kernel,
        grid=(n_ord, D // tn, N // tl),
        in_specs=[
            pl.BlockSpec((None, None, tn, Fh), lambda n, c, l: (n, l // half, c, 0)),
            pl.BlockSpec((tl, Fh), lambda n, c, l: (l, 0)),
            pl.BlockSpec((1, tl), lambda n, c, l: (0, l)),
            pl.BlockSpec((tn, 1), lambda n, c, l: (c, 0)),
        ],
        out_specs=pl.BlockSpec((None, tn, rows, FFT_N2), lambda n, c, l: (n, c, l, 0)),
        out_shape=jax.ShapeDtypeStruct((n_ord, D, N // FFT_N2, FFT_N2), BF16),
        compiler_params=_cparams("parallel", "parallel", "arbitrary"),
        name="hyena_filter_taps",
    )(wout_t, feats_all, t_all, deltas_col)


def _dft_constants(N):
    N2 = FFT_N2
    N1 = N // N2
    n1 = np.arange(N1)
    th1 = 2.0 * np.pi * np.outer(n1, n1) / N1
    fl = np.concatenate([np.cos(th1), -np.sin(th1)], axis=0)
    tht = 2.0 * np.pi * np.outer(n1, np.arange(N2)) / N
    n2 = np.arange(N2)
    th2 = 2.0 * np.pi * np.outer(n2, n2) / N2
    fr, fi = np.cos(th2), -np.sin(th2)
    ilc = np.cos(th1) / N
    ils = -np.sin(th1) / N
    bf = lambda a: jnp.asarray(a, dtype=F32).astype(BF16)
    return dict(
        fl_full=bf(fl), fl_half=bf(fl[:, :N1 // 2]),
        tr=jnp.asarray(np.cos(tht), F32), ti=jnp.asarray(-np.sin(tht), F32),
        fr=bf(fr), fi=bf(fi), nfi=bf(-fi),
        ilc=bf(ilc[:N1 // 2]), ils=bf(ils[:N1 // 2]),
    )


def _dft_forward(z_of, n_ch, n1, fl_ref, tr_ref, ti_ref, fr_ref, fi_ref, nfi_ref, br_ref, bi_ref):
    tr, ti = tr_ref[...], ti_ref[...]
    for c in range(n_ch):
        a = jnp.dot(fl_ref[...], z_of(c), preferred_element_type=F32)
        ar, ai = a[:n1], a[n1:]
        br_ref[c * n1:(c + 1) * n1, :] = (ar * tr - ai * ti).astype(BF16)
        bi_ref[c * n1:(c + 1) * n1, :] = (ar * ti + ai * tr).astype(BF16)
    br, bi = br_ref[...], bi_ref[...]
    xr = (jnp.dot(br, fr_ref[...], preferred_element_type=F32)
          + jnp.dot(bi, nfi_ref[...], preferred_element_type=F32))
    xi = (jnp.dot(br, fi_ref[...], preferred_element_type=F32)
          + jnp.dot(bi, fr_ref[...], preferred_element_type=F32))
    return xr, xi


def _filter_fft_kernel(kk_ref, fl_ref, tr_ref, ti_ref, fr_ref, fi_ref, nfi_ref,
                       kr_ref, ki_ref, br_ref, bi_ref):
    n1 = tr_ref.shape[0]
    n_ch = kk_ref.shape[0] // n1

    def z_of(c):
        kk = kk_ref[c * n1:(c + 1) * n1, :]
        l1 = jnp.sum(jnp.abs(kk.astype(F32)), axis=-1, keepdims=True)
        l1 = jnp.sum(l1, axis=0, keepdims=True)
        return (kk.astype(F32) * (1.0 / l1)).astype(BF16)

    xr, xi = _dft_forward(z_of, n_ch, n1, fl_ref, tr_ref, ti_ref, fr_ref, fi_ref, nfi_ref,
                          br_ref, bi_ref)
    kr_ref[...] = xr.astype(kr_ref.dtype)
    ki_ref[...] = xi.astype(ki_ref.dtype)


def _filter_fft(kk, dc):
    n_ord, D, N1, _ = kk.shape
    cb = _tile(D, FFT_CB)
    rows = cb * N1
    kk2 = kk.reshape(n_ord, D * N1, FFT_N2)
    const = lambda a: pl.BlockSpec(a.shape, lambda n, c: (0, 0))
    consts = [dc['fl_full'], dc['tr'], dc['ti'], dc['fr'], dc['fi'], dc['nfi']]
    blk = pl.BlockSpec((None, rows, FFT_N2), lambda n, c: (n, c, 0))
    return pl.pallas_call(
        _filter_fft_kernel,
        grid=(n_ord, D // cb),
        in_specs=[blk] + [const(a) for a in consts],
        out_specs=[blk, blk],
        out_shape=[jax.ShapeDtypeStruct((n_ord, D * N1, FFT_N2), BF16)] * 2,
        scratch_shapes=[pltpu.VMEM((rows, FFT_N2), BF16)] * 2,
        compiler_params=_cparams("parallel", "parallel"),
        name="hyena_filter_fft",
    )(kk2, *consts)


def _short_conv(u, p, k, n1h):
    rows, n2 = u.shape
    lane = lax.broadcasted_iota(jnp.int32, u.shape, 1)
    r1 = lax.broadcasted_iota(jnp.int32, u.shape, 0) % n1h
    back = pltpu.roll(u, 1, axis=1)
    prev = jnp.where(lane == 0,
                     jnp.where(r1 == 0, 0.0, pltpu.roll(back, 1, axis=0)), back)
    fwd = pltpu.roll(u, n2 - 1, axis=1)
    nxt = jnp.where(lane == n2 - 1,
                    jnp.where(r1 == n1h - 1, 0.0, pltpu.roll(fwd, rows - 1, axis=0)), fwd)
    c = 4 * k
    return (p[:, c:c + 1] * prev + p[:, c + 1:c + 2] * u + p[:, c + 2:c + 3] * nxt
            + p[:, c + 3:c + 4])


def _hyena_conv_kernel(uv_ref, ug0_ref, ug1_ref, p_ref, kr_ref, ki_ref,
                       fl_ref, tr_ref, ti_ref, fr_ref, fi_ref, nfi_ref, ilc_ref, ils_ref,
                       o_ref, br_ref, bi_ref, y_ref):
    n1 = tr_ref.shape[0]
    n1h = n1 // 2
    n_ch = uv_ref.shape[0] // n1h
    p = p_ref[...]
    tr, ti = tr_ref[...], ti_ref[...]
    gate_refs = (ug0_ref, ug1_ref)

    z = _short_conv(uv_ref[...], p, 0, n1h)
    for n in range(HYENA_ORDER):
        zb = z.astype(BF16)
        xr, xi = _dft_forward(lambda c: zb[c * n1h:(c + 1) * n1h, :], n_ch, n1,
                              fl_ref, tr_ref, ti_ref, fr_ref, fi_ref, nfi_ref, br_ref, bi_ref)
        kr, ki = kr_ref[n].astype(F32), ki_ref[n].astype(F32)
        yr = (xr * kr - xi * ki).astype(BF16)
        yi = (xr * ki + xi * kr).astype(BF16)
        gr = (jnp.dot(yr, fr_ref[...], preferred_element_type=F32)
              + jnp.dot(yi, fi_ref[...], preferred_element_type=F32))
        gi = (jnp.dot(yi, fr_ref[...], preferred_element_type=F32)
              + jnp.dot(yr, nfi_ref[...], preferred_element_type=F32))
        for c in range(n_ch):
            grc, gic = gr[c * n1:(c + 1) * n1], gi[c * n1:(c + 1) * n1]
            hr = (grc * tr + gic * ti).astype(BF16)
            hi = (gic * tr - grc * ti).astype(BF16)
            y_ref[c * n1h:(c + 1) * n1h, :] = (
                jnp.dot(ilc_ref[...], hr, preferred_element_type=F32)
                + jnp.dot(ils_ref[...], hi, preferred_element_type=F32))
        gate = _short_conv(gate_refs[n][...], p, n + 1, n1h)
        z = gate * (y_ref[...] + p[:, 12 + n:13 + n] * z)
    o_ref[...] = z.astype(o_ref.dtype)


def _hyena_conv(ut, params_rows, kr, ki, dc, D, L):
    N1 = 2 * L // FFT_N2
    n1h = N1 // 2
    cb = _tile(D, FFT_CB)
    rows = cb * n1h
    u3 = ut.reshape(3, D * n1h, FFT_N2)
    ublk = lambda part: pl.BlockSpec((None, rows, FFT_N2), lambda c: (part, c, 0))
    const = lambda a: pl.BlockSpec(a.shape, lambda c: (0, 0))
    consts = [dc['fl_half'], dc['tr'], dc['ti'], dc['fr'], dc['fi'], dc['nfi'], dc['ilc'], dc['ils']]
    kblk = pl.BlockSpec((HYENA_ORDER, cb * N1, FFT_N2), lambda c: (0, c, 0))
    out = pl.pallas_call(
        _hyena_conv_kernel,
        grid=(D // cb,),
        in_specs=[ublk(0), ublk(1), ublk(2),
                  pl.BlockSpec((rows, 16), lambda c: (c, 0)), kblk, kblk]
                 + [const(a) for a in consts],
        out_specs=pl.BlockSpec((rows, FFT_N2), lambda c: (c, 0)),
        out_shape=jax.ShapeDtypeStruct((D * n1h, FFT_N2), BF16),
        scratch_shapes=[pltpu.VMEM((cb * N1, FFT_N2), BF16)] * 2
                       + [pltpu.VMEM((rows, FFT_N2), F32)],
        compiler_params=_cparams("parallel"),
        name="hyena_long_conv",
    )(u3, u3, u3, params_rows, kr, ki, *consts)
    return out.reshape(D, L)


def _hyena_layer(x, g, shift, scale, gate, w_in, b_in, conv_w, conv_b, f_w1, f_b1, f_w2, f_b2,
                 f_w3, f_b3, f_freq, f_wout, skip, w_out, b_out):
    L, D = x.shape
    n1h = L // FFT_N2
    hn = _norm_mod(x, g, shift, scale)
    ut = _inproj_t(hn, w_in.T.astype(BF16), b_in[:, None])

    feats_all, t_all = _filter_features(L, f_w1, f_b1, f_w2, f_b2, f_w3, f_b3, f_freq)
    deltas = jnp.abs(jnp.linspace(math.log(DECAY_TARGET) / SLOW_DECAY_PCT,
                                  math.log(DECAY_TARGET) / FAST_DECAY_PCT, D, dtype=F32))
    wout_t = jnp.transpose(f_wout, (1, 2, 3, 0)).astype(BF16)
    kk = _filter_taps(wout_t, feats_all, t_all, deltas[:, None])
    dc = _dft_constants(2 * L)
    kr, ki = _filter_fft(kk, dc)

    cw = conv_w.reshape(SHORT_CONV, HYENA_ORDER + 1, D)
    cb = conv_b.reshape(HYENA_ORDER + 1, D)
    cols = []
    for part in range(HYENA_ORDER + 1):
        cols += [cw[0, part], cw[1, part], cw[2, part], cb[part]]
    cols += [skip[0], skip[1], jnp.zeros((D,), F32), jnp.zeros((D,), F32)]
    params_rows = jnp.repeat(jnp.stack(cols, axis=1), n1h, axis=0)

    zt = _hyena_conv(ut, params_rows, kr, ki, dc, D, L)
    return _matmul(zt.T, w_out.astype(BF16), bias=b_out[None, :], resid=x, gate=gate,
                   name="hyena_outproj")


def _attention_layer(x, hc, g, shift, scale, gate, cshift, cscale, w_qkv, w_out, q_gain, k_gain, rpb):
    S, D = x.shape
    rows = S // GRID_W
    w_bf16 = w_qkv.astype(BF16)
    gains = jnp.stack([q_gain * (HEAD_DIM ** -0.5), k_gain, jnp.ones_like(k_gain)])[:, None, :]
    qkvh = _qkv_proj(x, g, shift, scale, w_bf16, gains)
    ctxh = _qkv_proj(hc, g, cshift, cscale, w_bf16, gains)
    bias = _attention_bias(rpb, rows)
    o = _attention(qkvh, ctxh, bias, S, D)
    return _matmul(o, w_out.astype(BF16), resid=x, gate=gate, name="attn_outproj")


def _ada_rows(sc, scc, ada_w, ada_b, layer):
    D = sc.shape[-1]
    a = jnp.zeros((8, D), F32).at[0].set(sc[0]).at[1].set(scc).astype(BF16)
    return _matmul(a, ada_w, bias=ada_b[layer][None, :], tm=8, tn=512, name="adaln", w_layer=layer)


def kernel(x, c, ctx, c_ctx, ada_w, ada_b, norm_mix_g, norm_ffn_g, na_w_qkv, na_w_out, na_q_gain, na_k_gain, na_rpb, hy_w_in, hy_b_in, hy_conv_w, hy_conv_b, hy_f_w1, hy_f_b1, hy_f_w2, hy_f_b2, hy_f_w3, hy_f_b3, hy_f_freq, hy_f_wout, hy_skip, hy_w_out, hy_b_out, moe_w_router, moe_b_router, moe_w_gate, moe_w_up, moe_w_down, moe_ws_gate, moe_ws_up, moe_ws_down):
    B, S, D = x.shape
    assert B == 1 and D % HEAD_DIM == 0 and S % (ATT_ROWS * GRID_W) == 0
    depth = ada_w.shape[0]
    sc = jax.nn.silu(c)
    scc = jax.nn.silu(c_ctx)
    xs = x[0]
    hc = ctx[0]
    row = lambda v: v.reshape(1, D)
    for i in range(depth):
        j = i // 2
        ada = _ada_rows(sc, scc, ada_w, ada_b, i)
        sh_a, sc_a, g_a, sh_f, sc_f, g_f = [row(v) for v in jnp.split(ada[0], 6)]
        if i % 2 == 0:
            csh_a, csc_a = [row(v) for v in jnp.split(ada[1], 6)[:2]]
            xs = _attention_layer(xs, hc, row(norm_mix_g[i]), sh_a, sc_a, g_a, csh_a, csc_a,
                                  na_w_qkv[j], na_w_out[j], na_q_gain[j], na_k_gain[j],
                                  na_rpb[j])
        else:
            xs = _hyena_layer(xs, row(norm_mix_g[i]), sh_a, sc_a, g_a,
                              hy_w_in[j], hy_b_in[j], hy_conv_w[j], hy_conv_b[j],
                              hy_f_w1[j], hy_f_b1[j], hy_f_w2[j], hy_f_b2[j], hy_f_w3[j], hy_f_b3[j],
                              hy_f_freq[j], hy_f_wout[j], hy_skip[j], hy_w_out[j], hy_b_out[j])
        xs = _moe_layer(xs, row(norm_ffn_g[i]), sh_f, sc_f, g_f, i,
                        moe_w_router[i], moe_b_router[i], moe_w_gate, moe_w_up, moe_w_down,
                        moe_ws_gate, moe_ws_up, moe_ws_down)
    return xs[None]
```

```python
import functools
import math

import numpy as np
import jax
import jax.numpy as jnp
from jax import lax
from jax.experimental import pallas as pl
from jax.experimental.pallas import tpu as pltpu

F32 = jnp.float32
BF16 = jnp.bfloat16

GRID_W = 64
HEAD_DIM = 128
WIN_ROWS = 8
WIN_COLS = 16
SHORT_CONV = 3
HYENA_ORDER = 2
FILTER_EMB = 33
DECAY_TARGET = 1e-2
FAST_DECAY_PCT = 0.3
SLOW_DECAY_PCT = 1.5
N_EXPERTS = 64
N_GROUPS = 8
TOPK_GROUPS = 4
TOP_K = 8
ROUTED_SCALE = 2.5
MOE_BLOCK = 256
NORM_EPS = 1e-6

LANES = 128
VMEM_LIMIT_BYTES = 48 << 20
MASK_VALUE = -1e30

ATT_ROWS = 8
ATT_HALO = 4
FFT_N2 = 256
FFT_CB = 8
NT_DIMS = (((1,), (1,)), ((), ()))


def _cparams(*sem):
    return pltpu.CompilerParams(dimension_semantics=sem, vmem_limit_bytes=VMEM_LIMIT_BYTES)


def _tile(n, t):
    t = min(n, t)
    assert n % t == 0, (n, t)
    return t


def _norm_modulate(x_ref, g_ref, sh_ref, sc_ref, out_ref, chunk=32):
    g = g_ref[...]
    sc1 = 1.0 + sc_ref[...]
    sh = sh_ref[...]
    rows = x_ref.shape[0]
    chunk = min(chunk, rows)

    def body(r, carry):
        sl = pl.ds(pl.multiple_of(r * chunk, chunk), chunk)
        x = x_ref[sl, :]
        ms = jnp.mean(x * x, axis=-1, keepdims=True)
        y = (x * lax.rsqrt(ms + NORM_EPS)) * g
        out_ref[sl, :] = (y * sc1 + sh).astype(out_ref.dtype)
        return carry

    lax.fori_loop(0, rows // chunk, body, 0)


def _qkv_kernel(x_ref, g_ref, sh_ref, sc_ref, w_ref, gain_ref, o_ref, hn_ref, *, n_norm_tiles):
    j = pl.program_id(1)

    @pl.when(j == 0)
    def _():
        _norm_modulate(x_ref, g_ref, sh_ref, sc_ref, hn_ref)

    y = jnp.dot(hn_ref[...], w_ref[...], preferred_element_type=F32)
    heads = [y[:, hh * HEAD_DIM:(hh + 1) * HEAD_DIM] for hh in range(o_ref.shape[0])]

    @pl.when(j < n_norm_tiles)
    def _():
        gain = gain_ref[...]
        for hh, yh in enumerate(heads):
            ms = jnp.mean(yh * yh, axis=-1, keepdims=True)
            o_ref[hh] = ((yh * lax.rsqrt(ms + NORM_EPS)) * gain).astype(o_ref.dtype)

    @pl.when(j >= n_norm_tiles)
    def _():
        for hh, yh in enumerate(heads):
            o_ref[hh] = yh.astype(o_ref.dtype)


def _qkv_proj(x, g, shift, scale, w_bf16, gains, tm=512, tn=512):
    S, D = x.shape
    N = w_bf16.shape[1]
    tm, tn = _tile(S, tm), _tile(D, tn)
    tiles_per_region = D // tn
    return pl.pallas_call(
        functools.partial(_qkv_kernel, n_norm_tiles=2 * tiles_per_region),
        grid=(S // tm, N // tn),
        in_specs=[
            pl.BlockSpec((tm, D), lambda i, j: (i, 0)),
            pl.BlockSpec((1, D), lambda i, j: (0, 0)),
            pl.BlockSpec((1, D), lambda i, j: (0, 0)),
            pl.BlockSpec((1, D), lambda i, j: (0, 0)),
            pl.BlockSpec((D, tn), lambda i, j: (0, j)),
            pl.BlockSpec((None, 1, HEAD_DIM), lambda i, j: (j // tiles_per_region, 0, 0)),
        ],
        out_specs=pl.BlockSpec((tn // HEAD_DIM, tm, HEAD_DIM), lambda i, j: (j, i, 0)),
        out_shape=jax.ShapeDtypeStruct((N // HEAD_DIM, S, HEAD_DIM), BF16),
        scratch_shapes=[pltpu.VMEM((tm, D), BF16)],
        compiler_params=_cparams("parallel", "arbitrary"),
        name="qkv_proj",
    )(x, g, shift, scale, w_bf16, gains)


def _attn_kernel(q_ref, kp_ref, kc_ref, kn_ref, vp_ref, vc_ref, vn_ref, kx_ref, vx_ref,
                 b_ref, o_ref):
    halo = ATT_HALO * GRID_W
    blk = ATT_ROWS * GRID_W
    q = q_ref[...]
    k_parts = (kp_ref[blk - halo:, :], kc_ref[...], kn_ref[:halo, :], kx_ref[...])
    v_parts = (vp_ref[blk - halo:, :], vc_ref[...], vn_ref[:halo, :], vx_ref[...])
    bias_cols = (0, halo, halo + blk, 2 * halo + blk)

    scores = []
    for idx, k in enumerate(k_parts):
        s = lax.dot_general(q, k, NT_DIMS, preferred_element_type=F32)
        if idx < 3:
            s = s + b_ref[:, bias_cols[idx]:bias_cols[idx + 1]]
        scores.append(s)
    m = scores[0].max(axis=-1, keepdims=True)
    for s in scores[1:]:
        m = jnp.maximum(m, s.max(axis=-1, keepdims=True))
    den = jnp.zeros_like(m)
    acc = jnp.zeros(o_ref.shape, F32)
    for s, v in zip(scores, v_parts):
        p = jnp.exp(s - m)
        den = den + p.sum(axis=-1, keepdims=True)
        acc = acc + jnp.dot(p.astype(BF16), v, preferred_element_type=F32)
    o_ref[...] = (acc / den).astype(o_ref.dtype)


def _attention_bias(rpb, rows):
    n_rb = rows // ATT_ROWS
    rb_rep = np.array([0, min(1, n_rb - 1), n_rb - 1])
    rl = np.arange(ATT_ROWS)
    i = np.arange(ATT_ROWS + 2 * ATT_HALO)
    r = rb_rep[:, None] * ATT_ROWS + rl[None, :]
    a = rb_rep[:, None] * ATT_ROWS - ATT_HALO + i[None, :]
    rs = np.clip(r - WIN_ROWS // 2, 0, rows - WIN_ROWS)
    row_ok = (a[:, None, :] >= rs[:, :, None]) & (a[:, None, :] < rs[:, :, None] + WIN_ROWS)
    drow = np.clip(a[:, None, :] - r[:, :, None] + (WIN_ROWS - 1), 0, 2 * WIN_ROWS - 2)
    col = np.arange(GRID_W)
    cs = np.clip(col - WIN_COLS // 2, 0, GRID_W - WIN_COLS)
    col_ok = (col[None, :] >= cs[:, None]) & (col[None, :] < cs[:, None] + WIN_COLS)
    dcol = np.clip(col[None, :] - col[:, None] + (WIN_COLS - 1), 0, 2 * WIN_COLS - 2)
    tcol = jnp.where(col_ok[None, None], rpb[:, :, dcol], MASK_VALUE)
    val = jnp.take(tcol, drow.reshape(-1), axis=1)
    H = rpb.shape[0]
    val = val.reshape((H,) + drow.shape + (GRID_W, GRID_W))
    val = jnp.where(row_ok[None, :, :, :, None, None], val, MASK_VALUE)
    val = jnp.transpose(val, (1, 0, 2, 4, 3, 5))
    return val.reshape(3, H, ATT_ROWS * GRID_W, (ATT_ROWS + 2 * ATT_HALO) * GRID_W).astype(F32)


def _attention(qkvh, ctxh, bias, S, D):
    H = D // HEAD_DIM
    C = ctxh.shape[1]
    blk = ATT_ROWS * GRID_W
    n_rb = S // blk
    nkeys = (ATT_ROWS + 2 * ATT_HALO) * GRID_W

    def spec(head_off, shift):
        return pl.BlockSpec(
            (None, blk, HEAD_DIM),
            lambda h, rb: (head_off + h, jnp.clip(rb + shift, 0, n_rb - 1), 0))

    def variant(rb):
        return jnp.where(rb == 0, 0, jnp.where(rb == n_rb - 1, 2, 1))

    return pl.pallas_call(
        _attn_kernel,
        grid=(H, n_rb),
        in_specs=[
            spec(0, 0),
            spec(H, -1), spec(H, 0), spec(H, 1),
            spec(2 * H, -1), spec(2 * H, 0), spec(2 * H, 1),
            pl.BlockSpec((None, C, HEAD_DIM), lambda h, rb: (H + h, 0, 0)),
            pl.BlockSpec((None, C, HEAD_DIM), lambda h, rb: (2 * H + h, 0, 0)),
            pl.BlockSpec((None, None, blk, nkeys), lambda h, rb: (variant(rb), h, 0, 0)),
        ],
        out_specs=pl.BlockSpec((blk, HEAD_DIM), lambda h, rb: (rb, h)),
        out_shape=jax.ShapeDtypeStruct((S, D), BF16),
        compiler_params=_cparams("parallel", "arbitrary"),
        name="nbr_attention",
    )(qkvh, qkvh, qkvh, qkvh, qkvh, qkvh, qkvh, ctxh, ctxh, bias)


def _mm_kernel(*refs, has_bias, has_resid):
    a_ref, w_ref = refs[0], refs[1]
    o_ref = refs[-1]
    y = jnp.dot(a_ref[...], w_ref[...].astype(BF16), preferred_element_type=F32)
    k = 2
    if has_bias:
        y = y + refs[k][...]
        k += 1
    if has_resid:
        y = refs[k][...] + refs[k + 1][...] * y
    o_ref[...] = y.astype(o_ref.dtype)


def _matmul(a, w, bias=None, resid=None, gate=None, out_dtype=F32, tm=512, tn=512, name="matmul",
            w_layer=None):
    M, K = a.shape
    N = w.shape[-1]
    tm, tn = _tile(M, tm), _tile(N, tn)
    if w_layer is None:
        w_spec = pl.BlockSpec((K, tn), lambda i, j: (0, j))
    else:
        w_spec = pl.BlockSpec((None, K, tn), lambda i, j: (w_layer, 0, j))
    in_specs = [pl.BlockSpec((tm, K), lambda i, j: (i, 0)), w_spec]
    args = [a, w]
    if bias is not None:
        in_specs.append(pl.BlockSpec((1, tn), lambda i, j: (0, j)))
        args.append(bias)
    if resid is not None:
        in_specs += [pl.BlockSpec((tm, tn), lambda i, j: (i, j)),
                     pl.BlockSpec((1, tn), lambda i, j: (0, j))]
        args += [resid, gate]
    return pl.pallas_call(
        functools.partial(_mm_kernel, has_bias=bias is not None, has_resid=resid is not None),
        grid=(M // tm, N // tn),
        in_specs=in_specs,
        out_specs=pl.BlockSpec((tm, tn), lambda i, j: (i, j)),
        out_shape=jax.ShapeDtypeStruct((M, N), out_dtype),
        compiler_params=_cparams("parallel", "arbitrary"),
        name=name,
    )(*args)


def _route_tile(logits, b_row):
    neg = -jnp.inf
    lane = lax.broadcasted_iota(jnp.int32, logits.shape, 1)
    lane_f = lane.astype(F32)
    grp = lane // (N_EXPERTS // N_GROUPS)
    scores = jax.nn.sigmoid(logits)
    choice = jnp.where(lane < N_EXPERTS, scores + b_row, neg)

    def first_max(c):
        m = jnp.max(c, axis=1, keepdims=True)
        idx = jnp.min(jnp.where(c == m, lane_f, float(LANES)), axis=1, keepdims=True)
        return m, idx

    grp_score = []
    for gi in range(N_GROUPS):
        cg = jnp.where(grp == gi, choice, neg)
        m1, i1 = first_max(cg)
        m2 = jnp.max(jnp.where(lane_f == i1, neg, cg), axis=1, keepdims=True)
        grp_score.append(m1 + m2)
    c = jnp.full(logits.shape, neg, F32)
    for gi in range(N_GROUPS):
        rank = jnp.zeros_like(grp_score[gi])
        for gj in range(N_GROUPS):
            if gj < gi:
                ahead = grp_score[gj] >= grp_score[gi]
            elif gj > gi:
                ahead = grp_score[gj] > grp_score[gi]
            else:
                continue
            rank = rank + jnp.where(ahead, 1.0, 0.0)
        c = jnp.where(grp == gi, jnp.where(rank < TOPK_GROUPS, choice, neg), c)

    ids = jnp.zeros(logits.shape, F32)
    wts = jnp.zeros(logits.shape, F32)
    total = jnp.zeros_like(grp_score[0])
    for k in range(TOP_K):
        _, idx = first_max(c)
        hit = lane_f == idx
        s = jnp.sum(jnp.where(hit, scores, 0.0), axis=1, keepdims=True)
        ids = jnp.where(lane == k, idx, ids)
        wts = jnp.where(lane == k, s, wts)
        total = total + s
        c = jnp.where(hit, neg, c)
    return ids, wts / total * ROUTED_SCALE


def _router_kernel(x_ref, g_ref, sh_ref, sc_ref, w_ref, b_ref, t_ref, e_ref, p_ref, hn_ref):
    _norm_modulate(x_ref, g_ref, sh_ref, sc_ref, hn_ref)
    hn = hn_ref[...]
    t_ref[...] = hn.astype(t_ref.dtype)
    logits = jnp.dot(hn, w_ref[...], preferred_element_type=F32, precision=lax.Precision.HIGHEST)
    ids, wts = _route_tile(logits, b_ref[...])
    e_ref[...] = ids.astype(jnp.int32)
    p_ref[...] = wts


def _router(x, g, shift, scale, w_router_pad, b_router_pad, tm=256):
    T, D = x.shape
    NP = w_router_pad.shape[1]
    tm = _tile(T, tm)
    return pl.pallas_call(
        _router_kernel,
        grid=(T // tm,),
        in_specs=[
            pl.BlockSpec((tm, D), lambda i: (i, 0)),
            pl.BlockSpec((1, D), lambda i: (0, 0)),
            pl.BlockSpec((1, D), lambda i: (0, 0)),
            pl.BlockSpec((1, D), lambda i: (0, 0)),
            pl.BlockSpec((D, NP), lambda i: (0, 0)),
            pl.BlockSpec((1, NP), lambda i: (0, 0)),
        ],
        out_specs=[pl.BlockSpec((tm, D), lambda i: (i, 0)),
                   pl.BlockSpec((tm, NP), lambda i: (i, 0)),
                   pl.BlockSpec((tm, NP), lambda i: (i, 0))],
        out_shape=[jax.ShapeDtypeStruct((T, D), BF16),
                   jax.ShapeDtypeStruct((T, NP), jnp.int32),
                   jax.ShapeDtypeStruct((T, NP), F32)],
        scratch_shapes=[pltpu.VMEM((tm, D), F32)],
        compiler_params=_cparams("parallel"),
        name="moe_router",
    )(x, g, shift, scale, w_router_pad, b_router_pad)


def _expert_kernel(be_ref, nused_ref, x_ref, wg_ref, wu_ref, wd_ref, y_ref):
    b = pl.program_id(0)

    @pl.when(b < nused_ref[0])
    def _():
        x = x_ref[...]
        gt = lax.dot_general(x, wg_ref[...].astype(BF16), NT_DIMS, preferred_element_type=F32)
        up = lax.dot_general(x, wu_ref[...].astype(BF16), NT_DIMS, preferred_element_type=F32)
        h = (gt * jax.nn.sigmoid(gt)) * up
        y = jnp.dot(h.astype(BF16), wd_ref[...].astype(BF16), preferred_element_type=F32)
        y_ref[...] = y.astype(y_ref.dtype)

    @pl.when(b >= nused_ref[0])
    def _():
        y_ref[...] = jnp.zeros(y_ref.shape, y_ref.dtype)


def _experts(xg, wg, wu, wd, layer, block_e, n_used):
    R, D = xg.shape
    Hd = wg.shape[-2]
    n_blocks = R // MOE_BLOCK
    once = pl.Buffered(1)
    return pl.pallas_call(
        _expert_kernel,
        grid_spec=pltpu.PrefetchScalarGridSpec(
            num_scalar_prefetch=2,
            grid=(n_blocks,),
            in_specs=[
                pl.BlockSpec((MOE_BLOCK, D), lambda b, be, nu: (b, 0)),
                pl.BlockSpec((None, None, Hd, D), lambda b, be, nu: (layer, be[b], 0, 0),
                             pipeline_mode=once),
                pl.BlockSpec((None, None, Hd, D), lambda b, be, nu: (layer, be[b], 0, 0),
                             pipeline_mode=once),
                pl.BlockSpec((None, None, Hd, D), lambda b, be, nu: (layer, be[b], 0, 0),
                             pipeline_mode=once),
            ],
            out_specs=pl.BlockSpec((MOE_BLOCK, D), lambda b, be, nu: (b, 0)),
        ),
        out_shape=jax.ShapeDtypeStruct((R, D), BF16),
        compiler_params=_cparams("arbitrary"),
        name="moe_experts",
    )(block_e, n_used, xg, wg, wu, wd)


def _combine_kernel(x_ref, ysh_ref, yg_ref, w_ref, gate_ref, o_ref):
    acc = ysh_ref[...].astype(F32)
    w = w_ref[...]
    for k in range(yg_ref.shape[0]):
        acc = acc + w[:, k:k + 1] * yg_ref[k].astype(F32)
    o_ref[...] = x_ref[...] + gate_ref[...] * acc


def _combine(x, ysh, yg, top_w, gate, tm=128):
    T, D = x.shape
    tm = _tile(T, tm)
    return pl.pallas_call(
        _combine_kernel,
        grid=(T // tm,),
        in_specs=[
            pl.BlockSpec((tm, D), lambda i: (i, 0)),
            pl.BlockSpec((tm, D), lambda i: (i, 0)),
            pl.BlockSpec((yg.shape[0], tm, D), lambda i: (0, i, 0)),
            pl.BlockSpec((tm, top_w.shape[1]), lambda i: (i, 0)),
            pl.BlockSpec((1, D), lambda i: (0, 0)),
        ],
        out_specs=pl.BlockSpec((tm, D), lambda i: (i, 0)),
        out_shape=jax.ShapeDtypeStruct((T, D), F32),
        compiler_params=_cparams("parallel"),
        name="moe_combine",
    )(x, ysh, yg, top_w, gate)


def _slot_kernel(eid_ref, pstart_ref, dest_ref, next_ref):
    n, _, ch = eid_ref.shape
    n_exp = pstart_ref.shape[0]
    next_ref[...] = pstart_ref[...]
    e_iota = lax.broadcasted_iota(jnp.int32, (n_exp, ch), 0)
    earlier = (lax.broadcasted_iota(jnp.int32, (ch, ch), 0)
               < lax.broadcasted_iota(jnp.int32, (ch, ch), 1)).astype(BF16)

    def body(r, carry):
        onehot = (eid_ref[r] == e_iota).astype(F32)
        before = jnp.dot(onehot.astype(BF16), earlier, preferred_element_type=F32)
        slot = jnp.sum(onehot * (next_ref[...] + before), axis=0, keepdims=True)
        dest_ref[r] = slot.astype(jnp.int32)
        next_ref[...] = next_ref[...] + jnp.sum(onehot, axis=1, keepdims=True)
        return carry

    lax.fori_loop(0, n, body, 0)


def _dispatch_plan(top_e, ch=256):
    T = top_e.shape[0]
    n_assign = T * TOP_K
    eid = top_e.reshape(-1).astype(jnp.int32)
    counts = jnp.sum((eid[:, None] == jnp.arange(N_EXPERTS)).astype(jnp.int32), axis=0)
    padded = (counts + MOE_BLOCK - 1) // MOE_BLOCK * MOE_BLOCK
    pend = jnp.cumsum(padded)
    pstart = pend - padded
    n_blocks = -(-n_assign // MOE_BLOCK) + N_EXPERTS
    n_rows = n_blocks * MOE_BLOCK
    ch = _tile(n_assign, ch)
    dest = pl.pallas_call(
        _slot_kernel,
        out_shape=jax.ShapeDtypeStruct((n_assign // ch, 1, ch), jnp.int32),
        scratch_shapes=[pltpu.VMEM((N_EXPERTS, 1), F32)],
        compiler_params=pltpu.CompilerParams(vmem_limit_bytes=VMEM_LIMIT_BYTES),
        name="moe_slots",
    )(eid.reshape(n_assign // ch, 1, ch), pstart.astype(F32)[:, None])
    dest = dest.reshape(-1)
    tok = jnp.repeat(jnp.arange(T, dtype=jnp.int32), TOP_K)
    tok_buf = jnp.zeros((n_rows,), jnp.int32).at[dest].set(tok)
    block_start = jnp.arange(n_blocks, dtype=jnp.int32) * MOE_BLOCK
    block_e = jnp.minimum(jnp.sum((pend[None, :] <= block_start[:, None]).astype(jnp.int32), axis=1),
                          N_EXPERTS - 1).astype(jnp.int32)
    n_used = (pend[-1] // MOE_BLOCK).astype(jnp.int32).reshape(1)
    return dest.reshape(T, TOP_K), tok_buf, block_e, n_used


def _moe_layer(x, g, shift, scale, gate, layer, w_router, b_router, w_gate, w_up, w_down,
               ws_gate, ws_up, ws_down):
    T, D = x.shape
    w_router_pad = jnp.zeros((D, LANES), F32).at[:, :N_EXPERTS].set(w_router)
    b_router_pad = jnp.zeros((1, LANES), F32).at[0, :N_EXPERTS].set(b_router)
    t_bf16, top_e, top_w = _router(x, g, shift, scale, w_router_pad, b_router_pad)
    top_e, top_w = top_e[:, :TOP_K], top_w[:, :TOP_K]
    pos, tok_buf, block_e, n_used = _dispatch_plan(top_e)

    wg_t, wu_t = jnp.swapaxes(w_gate, -1, -2), jnp.swapaxes(w_up, -1, -2)
    wsg_t, wsu_t = jnp.swapaxes(ws_gate, -1, -2)[:, None], jnp.swapaxes(ws_up, -1, -2)[:, None]
    xg = t_bf16[tok_buf]
    y = _experts(xg, wg_t, wu_t, w_down, layer, block_e, n_used)
    n_sh = T // MOE_BLOCK
    ysh = _experts(t_bf16, wsg_t, wsu_t, ws_down[:, None], layer,
                   jnp.zeros((n_sh,), jnp.int32), jnp.full((1,), n_sh, jnp.int32))
    yg = y[pos.T]
    return _combine(x, ysh, yg, top_w, gate)


def _norm_mod_kernel(x_ref, g_ref, sh_ref, sc_ref, o_ref):
    _norm_modulate(x_ref, g_ref, sh_ref, sc_ref, o_ref)


def _norm_mod(x, g, shift, scale, tm=256):
    S, D = x.shape
    tm = _tile(S, tm)
    vec = pl.BlockSpec((1, D), lambda i: (0, 0))
    return pl.pallas_call(
        _norm_mod_kernel,
        grid=(S // tm,),
        in_specs=[pl.BlockSpec((tm, D), lambda i: (i, 0)), vec, vec, vec],
        out_specs=pl.BlockSpec((tm, D), lambda i: (i, 0)),
        out_shape=jax.ShapeDtypeStruct((S, D), BF16),
        compiler_params=_cparams("parallel"),
        name="norm_modulate",
    )(x, g, shift, scale)


def _inproj_t_kernel(wt_ref, hn_ref, b_ref, o_ref, *, groups):
    y = lax.dot_general(wt_ref[...], hn_ref[...], NT_DIMS, preferred_element_type=F32) + b_ref[...]
    per = y.shape[1] // FFT_N2
    i = pl.program_id(1)
    for s in range(groups):
        @pl.when(i % groups == s)
        def _():
            for q in range(per):
                o_ref[:, s * per + q, :] = y[:, q * FFT_N2:(q + 1) * FFT_N2]


def _inproj_t(hn, wt_bf16, bias_col, tm=512, tn=512):
    S, D = hn.shape
    N = wt_bf16.shape[0]
    tm, tn = _tile(S, tm), _tile(N, tn)
    n_chunks = S // FFT_N2
    per = tm // FFT_N2
    rows = min(8, n_chunks)
    groups = rows // per
    return pl.pallas_call(
        functools.partial(_inproj_t_kernel, groups=groups),
        grid=(N // tn, S // tm),
        in_specs=[
            pl.BlockSpec((tn, D), lambda j, i: (j, 0)),
            pl.BlockSpec((tm, D), lambda j, i: (i, 0)),
            pl.BlockSpec((tn, 1), lambda j, i: (j, 0)),
        ],
        out_specs=pl.BlockSpec((tn, rows, FFT_N2), lambda j, i: (j, i // groups, 0)),
        out_shape=jax.ShapeDtypeStruct((N, n_chunks, FFT_N2), F32),
        compiler_params=_cparams("parallel", "arbitrary"),
        name="hyena_inproj",
    )(wt_bf16, hn, bias_col)


def _filter_mlp_kernel(z_ref, w1_ref, b1_ref, w2_ref, b2_ref, w3_ref, b3_ref, fq_ref, o_ref):
    hp = lax.Precision.HIGHEST
    fq = fq_ref[...]
    h = jnp.sin(fq * (jnp.dot(z_ref[...], w1_ref[...], precision=hp,
                              preferred_element_type=F32) + b1_ref[...]))
    h = jnp.sin(fq * (jnp.dot(h, w2_ref[...], precision=hp,
                              preferred_element_type=F32) + b2_ref[...]))
    h = jnp.sin(fq * (jnp.dot(h, w3_ref[...], precision=hp,
                              preferred_element_type=F32) + b3_ref[...]))
    o_ref[...] = h.astype(o_ref.dtype)


def _filter_features(L, w1, b1, w2, b2, w3, b3, freq, tl=2048):
    Fh = w1.shape[1]
    t = jnp.linspace(0.0, 1.0, L, dtype=F32)[:, None]
    bands = (FILTER_EMB - 1) // 2
    f = jnp.linspace(1e-4, bands - 1, bands, dtype=F32)
    ang = (2.0 * math.pi / L) * jnp.arange(L, dtype=F32)[:, None] * f
    z = jnp.concatenate([t, jnp.cos(ang), -jnp.sin(ang)], axis=-1)
    zp = jnp.zeros((L, LANES), F32).at[:, :FILTER_EMB].set(z)
    w1p = jnp.zeros((LANES, Fh), F32).at[:FILTER_EMB].set(w1)
    tl = _tile(L, tl)
    row = lambda v: v.reshape(1, -1)
    const = lambda shape: pl.BlockSpec(shape, lambda i: (0, 0))
    feats = pl.pallas_call(
        _filter_mlp_kernel,
        grid=(L // tl,),
        in_specs=[pl.BlockSpec((tl, LANES), lambda i: (i, 0)),
                  const((LANES, Fh)), const((1, Fh)), const((Fh, Fh)), const((1, Fh)),
                  const((Fh, Fh)), const((1, Fh)), const((1, Fh))],
        out_specs=pl.BlockSpec((tl, Fh), lambda i: (i, 0)),
        out_shape=jax.ShapeDtypeStruct((L, Fh), BF16),
        compiler_params=_cparams("parallel"),
        name="hyena_filter_mlp",
    )(zp, w1p, row(b1), w2, row(b2), w3, row(b3), row(freq))
    zero = jnp.zeros((1, Fh), BF16)
    feats_all = jnp.concatenate([feats, zero, feats[:0:-1]], axis=0)
    t_all = jnp.concatenate([t[:, 0], t[:1, 0], t[:0:-1, 0]])[None, :]
    return feats_all, t_all


def _filter_taps_kernel(wt_ref, f_ref, t_ref, d_ref, o_ref):
    h = lax.dot_general(wt_ref[...], f_ref[...], NT_DIMS, preferred_element_type=F32)
    h = (h * jnp.exp(-(d_ref[...] * t_ref[...]))).astype(o_ref.dtype)
    for q in range(o_ref.shape[1]):
        o_ref[:, q, :] = h[:, q * FFT_N2:(q + 1) * FFT_N2]


def _filter_taps(wout_t, feats_all, t_all, deltas_col, tn=512):
    n_ord, _, D, Fh = wout_t.shape
    N = feats_all.shape[0]
    rows = 8
    tl = rows * FFT_N2
    assert (N // 2) % tl == 0, (N, tl)
    tn = _tile(D, tn)
    half = (N // 2) // tl
    return pl.pallas_call(
        _filter_taps_kernel,
        grid=(n_ord, D // tn, N // tl),
        in_specs=[
            pl.BlockSpec((None, None, tn, Fh), lambda n, c, l: (n, l // half, c, 0)),
            pl.BlockSpec((tl, Fh), lambda n, c, l: (l, 0)),
            pl.BlockSpec((1, tl), lambda n, c, l: (0, l)),
            pl.BlockSpec((tn, 1), lambda n, c, l: (c, 0)),
        ],
        out_specs=pl.BlockSpec((None, tn, rows, FFT_N2), lambda n, c, l: (n, c, l, 0)),
        out_shape=jax.ShapeDtypeStruct((n_ord, D, N // FFT_N2, FFT_N2), BF16),
        compiler_params=_cparams("parallel", "parallel", "arbitrary"),
        name="hyena_filter_taps",
    )(wout_t, feats_all, t_all, deltas_col)


def _dft_constants(N):
    N2 = FFT_N2
    N1 = N // N2
    n1 = np.arange(N1)
    th1 = 2.0 * np.pi * np.outer(n1, n1) / N1
    fl = np.concatenate([np.cos(th1), -np.sin(th1)], axis=0)
    tht = 2.0 * np.pi * np.outer(n1, np.arange(N2)) / N
    n2 = np.arange(N2)
    th2 = 2.0 * np.pi * np.outer(n2, n2) / N2
    fr, fi = np.cos(th2), -np.sin(th2)
    ilc = np.cos(th1) / N
    ils = -np.sin(th1) / N
    bf = lambda a: jnp.asarray(a, dtype=F32).astype(BF16)
    return dict(
        fl_full=bf(fl), fl_half=bf(fl[:, :N1 // 2]),
        tr=jnp.asarray(np.cos(tht), F32), ti=jnp.asarray(-np.sin(tht), F32),
        fr=bf(fr), fi=bf(fi), nfi=bf(-fi),
        ilc=bf(ilc[:N1 // 2]), ils=bf(ils[:N1 // 2]),
    )


def _dft_forward(z_of, n_ch, n1, fl_ref, tr_ref, ti_ref, fr_ref, fi_ref, nfi_ref, br_ref, bi_ref):
    tr, ti = tr_ref[...], ti_ref[...]
    for c in range(n_ch):
        a = jnp.dot(fl_ref[...], z_of(c), preferred_element_type=F32)
        ar, ai = a[:n1], a[n1:]
        br_ref[c * n1:(c + 1) * n1, :] = (ar * tr - ai * ti).astype(BF16)
        bi_ref[c * n1:(c + 1) * n1, :] = (ar * ti + ai * tr).astype(BF16)
    br, bi = br_ref[...], bi_ref[...]
    xr = (jnp.dot(br, fr_ref[...], preferred_element_type=F32)
          + jnp.dot(bi, nfi_ref[...], preferred_element_type=F32))
    xi = (jnp.dot(br, fi_ref[...], preferred_element_type=F32)
          + jnp.dot(bi, fr_ref[...], preferred_element_type=F32))
    return xr, xi


def _filter_fft_kernel(kk_ref, fl_ref, tr_ref, ti_ref, fr_ref, fi_ref, nfi_ref,
                       kr_ref, ki_ref, br_ref, bi_ref):
    n1 = tr_ref.shape[0]
    n_ch = kk_ref.shape[0] // n1

    def z_of(c):
        kk = kk_ref[c * n1:(c + 1) * n1, :]
        l1 = jnp.sum(jnp.abs(kk.astype(F32)), axis=-1, keepdims=True)
        l1 = jnp.sum(l1, axis=0, keepdims=True)
        return (kk.astype(F32) * (1.0 / l1)).astype(BF16)

    xr, xi = _dft_forward(z_of, n_ch, n1, fl_ref, tr_ref, ti_ref, fr_ref, fi_ref, nfi_ref,
                          br_ref, bi_ref)
    kr_ref[...] = xr.astype(kr_ref.dtype)
    ki_ref[...] = xi.astype(ki_ref.dtype)


def _filter_fft(kk, dc):
    n_ord, D, N1, _ = kk.shape
    cb = _tile(D, FFT_CB)
    rows = cb * N1
    kk2 = kk.reshape(n_ord, D * N1, FFT_N2)
    const = lambda a: pl.BlockSpec(a.shape, lambda n, c: (0, 0))
    consts = [dc['fl_full'], dc['tr'], dc['ti'], dc['fr'], dc['fi'], dc['nfi']]
    blk = pl.BlockSpec((None, rows, FFT_N2), lambda n, c: (n, c, 0))
    return pl.pallas_call(
        _filter_fft_kernel,
        grid=(n_ord, D // cb),
        in_specs=[blk] + [const(a) for a in consts],
        out_specs=[blk, blk],
        out_shape=[jax.ShapeDtypeStruct((n_ord, D * N1, FFT_N2), BF16)] * 2,
        scratch_shapes=[pltpu.VMEM((rows, FFT_N2), BF16)] * 2,
        compiler_params=_cparams("parallel", "parallel"),
        name="hyena_filter_fft",
    )(kk2, *consts)


def _short_conv(u, p, k, n1h):
    rows, n2 = u.shape
    lane = lax.broadcasted_iota(jnp.int32, u.shape, 1)
    r1 = lax.broadcasted_iota(jnp.int32, u.shape, 0) % n1h
    back = pltpu.roll(u, 1, axis=1)
    prev = jnp.where(lane == 0,
                     jnp.where(r1 == 0, 0.0, pltpu.roll(back, 1, axis=0)), back)
    fwd = pltpu.roll(u, n2 - 1, axis=1)
    nxt = jnp.where(lane == n2 - 1,
                    jnp.where(r1 == n1h - 1, 0.0, pltpu.roll(fwd, rows - 1, axis=0)), fwd)
    c = 4 * k
    return (p[:, c:c + 1] * prev + p[:, c + 1:c + 2] * u + p[:, c + 2:c + 3] * nxt
            + p[:, c + 3:c + 4])


def _hyena_conv_kernel(uv_ref, ug0_ref, ug1_ref, p_ref, kr_ref, ki_ref,
                       fl_ref, tr_ref, ti_ref, fr_ref, fi_ref, nfi_ref, ilc_ref, ils_ref,
                       o_ref, br_ref, bi_ref, y_ref):
    n1 = tr_ref.shape[0]
    n1h = n1 // 2
    n_ch = uv_ref.shape[0] // n1h
    p = p_ref[...]
    tr, ti = tr_ref[...], ti_ref[...]
    gate_refs = (ug0_ref, ug1_ref)

    z = _short_conv(uv_ref[...], p, 0, n1h)
    for n in range(HYENA_ORDER):
        zb = z.astype(BF16)
        xr, xi = _dft_forward(lambda c: zb[c * n1h:(c + 1) * n1h, :], n_ch, n1,
                              fl_ref, tr_ref, ti_ref, fr_ref, fi_ref, nfi_ref, br_ref, bi_ref)
        kr, ki = kr_ref[n].astype(F32), ki_ref[n].astype(F32)
        yr = (xr * kr - xi * ki).astype(BF16)
        yi = (xr * ki + xi * kr).astype(BF16)
        gr = (jnp.dot(yr, fr_ref[...], preferred_element_type=F32)
              + jnp.dot(yi, fi_ref[...], preferred_element_type=F32))
        gi = (jnp.dot(yi, fr_ref[...], preferred_element_type=F32)
              + jnp.dot(yr, nfi_ref[...], preferred_element_type=F32))
        for c in range(n_ch):
            grc, gic = gr[c * n1:(c + 1) * n1], gi[c * n1:(c + 1) * n1]
            hr = (grc * tr + gic * ti).astype(BF16)
            hi = (gic * tr - grc * ti).astype(BF16)
            y_ref[c * n1h:(c + 1) * n1h, :] = (
                jnp.dot(ilc_ref[...], hr, preferred_element_type=F32)
                + jnp.dot(ils_ref[...], hi, preferred_element_type=F32))
        gate = _short_conv(gate_refs[n][...], p, n + 1, n1h)
        z = gate * (y_ref[...] + p[:, 12 + n:13 + n] * z)
    o_ref[...] = z.astype(o_ref.dtype)


def _hyena_conv(ut, params_rows, kr, ki, dc, D, L):
    N1 = 2 * L // FFT_N2
    n1h = N1 // 2
    cb = _tile(D, FFT_CB)
    rows = cb * n1h
    u3 = ut.reshape(3, D * n1h, FFT_N2)
    ublk = lambda part: pl.BlockSpec((None, rows, FFT_N2), lambda c: (part, c, 0))
    const = lambda a: pl.BlockSpec(a.shape, lambda c: (0, 0))
    consts = [dc['fl_half'], dc['tr'], dc['ti'], dc['fr'], dc['fi'], dc['nfi'], dc['ilc'], dc['ils']]
    kblk = pl.BlockSpec((HYENA_ORDER, cb * N1, FFT_N2), lambda c: (0, c, 0))
    out = pl.pallas_call(
        _hyena_conv_kernel,
        grid=(D // cb,),
        in_specs=[ublk(0), ublk(1), ublk(2),
                  pl.BlockSpec((rows, 16), lambda c: (c, 0)), kblk, kblk]
                 + [const(a) for a in consts],
        out_specs=pl.BlockSpec((rows, FFT_N2), lambda c: (c, 0)),
        out_shape=jax.ShapeDtypeStruct((D * n1h, FFT_N2), BF16),
        scratch_shapes=[pltpu.VMEM((cb * N1, FFT_N2), BF16)] * 2
                       + [pltpu.VMEM((rows, FFT_N2), F32)],
        compiler_params=_cparams("parallel"),
        name="hyena_long_conv",
    )(u3, u3, u3, params_rows, kr, ki, *consts)
    return out.reshape(D, L)


def _hyena_layer(x, g, shift, scale, gate, w_in, b_in, conv_w, conv_b, f_w1, f_b1, f_w2, f_b2,
                 f_w3, f_b3, f_freq, f_wout, skip, w_out, b_out):
    L, D = x.shape
    n1h = L // FFT_N2
    hn = _norm_mod(x, g, shift, scale)
    ut = _inproj_t(hn, w_in.T.astype(BF16), b_in[:, None])

    feats_all, t_all = _filter_features(L, f_w1, f_b1, f_w2, f_b2, f_w3, f_b3, f_freq)
    deltas = jnp.abs(jnp.linspace(math.log(DECAY_TARGET) / SLOW_DECAY_PCT,
                                  math.log(DECAY_TARGET) / FAST_DECAY_PCT, D, dtype=F32))
    wout_t = jnp.transpose(f_wout, (1, 2, 3, 0)).astype(BF16)
    kk = _filter_taps(wout_t, feats_all, t_all, deltas[:, None])
    dc = _dft_constants(2 * L)
    kr, ki = _filter_fft(kk, dc)

    cw = conv_w.reshape(SHORT_CONV, HYENA_ORDER + 1, D)
    cb = conv_b.reshape(HYENA_ORDER + 1, D)
    cols = []
    for part in range(HYENA_ORDER + 1):
        cols += [cw[0, part], cw[1, part], cw[2, part], cb[part]]
    cols += [skip[0], skip[1], jnp.zeros((D,), F32), jnp.zeros((D,), F32)]
    params_rows = jnp.repeat(jnp.stack(cols, axis=1), n1h, axis=0)

    zt = _hyena_conv(ut, params_rows, kr, ki, dc, D, L)
    return _matmul(zt.T, w_out.astype(BF16), bias=b_out[None, :], resid=x, gate=gate,
                   name="hyena_outproj")


def _attention_layer(x, hc, g, shift, scale, gate, cshift, cscale, w_qkv, w_out, q_gain, k_gain, rpb):
    S, D = x.shape
    rows = S // GRID_W
    w_bf16 = w_qkv.astype(BF16)
    gains = jnp.stack([q_gain * (HEAD_DIM ** -0.5), k_gain, jnp.ones_like(k_gain)])[:, None, :]
    qkvh = _qkv_proj(x, g, shift, scale, w_bf16, gains)
    ctxh = _qkv_proj(hc, g, cshift, cscale, w_bf16, gains)
    bias = _attention_bias(rpb, rows)
    o = _attention(qkvh, ctxh, bias, S, D)
    return _matmul(o, w_out.astype(BF16), resid=x, gate=gate, name="attn_outproj")


def _ada_rows(sc, scc, ada_w, ada_b, layer):
    D = sc.shape[-1]
    a = jnp.zeros((8, D), F32).at[0].set(sc[0]).at[1].set(scc).astype(BF16)
    return _matmul(a, ada_w, bias=ada_b[layer][None, :], tm=8, tn=512, name="adaln", w_layer=layer)


def kernel(x, c, ctx, c_ctx, ada_w, ada_b, norm_mix_g, norm_ffn_g, na_w_qkv, na_w_out, na_q_gain, na_k_gain, na_rpb, hy_w_in, hy_b_in, hy_conv_w, hy_conv_b, hy_f_w1, hy_f_b1, hy_f_w2, hy_f_b2, hy_f_w3, hy_f_b3, hy_f_freq, hy_f_wout, hy_skip, hy_w_out, hy_b_out, moe_w_router, moe_b_router, moe_w_gate, moe_w_up, moe_w_down, moe_ws_gate, moe_ws_up, moe_ws_down):
    B, S, D = x.shape
    assert B == 1 and D % HEAD_DIM == 0 and S % (ATT_ROWS * GRID_W) == 0
    depth = ada_w.shape[0]
    sc = jax.nn.silu(c)
    scc = jax.nn.silu(c_ctx)
    xs = x[0]
    hc = ctx[0]
    row = lambda v: v.reshape(1, D)
    for i in range(depth):
        j = i // 2
        ada = _ada_rows(sc, scc, ada_w, ada_b, i)
        sh_a, sc_a, g_a, sh_f, sc_f, g_f = [row(v) for v in jnp.split(ada[0], 6)]
        if i % 2 == 0:
            csh_a, csc_a = [row(v) for v in jnp.split(ada[1], 6)[:2]]
            xs = _attention_layer(xs, hc, row(norm_mix_g[i]), sh_a, sc_a, g_a, csh_a, csc_a,
                                  na_w_qkv[j], na_w_out[j], na_q_gain[j], na_k_gain[j],
                                  na_rpb[j])
        else:
            xs = _hyena_layer(xs, row(norm_mix_g[i]), sh_a, sc_a, g_a,
                              hy_w_in[j], hy_b_in[j], hy_conv_w[j], hy_conv_b[j],
                              hy_f_w1[j], hy_f_b1[j], hy_f_w2[j], hy_f_b2[j], hy_f_w3[j], hy_f_b3[j],
                              hy_f_freq[j], hy_f_wout[j], hy_skip[j], hy_w_out[j], hy_b_out[j])
        xs = _moe_layer(xs, row(norm_ffn_g[i]), sh_f, sc_f, g_f, i,
                        moe_w_router[i], moe_b_router[i], moe_w_gate, moe_w_up, moe_w_down,
                        moe_ws_gate, moe_ws_up, moe_ws_down)
    return xs[None]
```

```python
import functools
import math

import numpy as np
import jax
import jax.numpy as jnp
from jax import lax
from jax.experimental import pallas as pl
from jax.experimental.pallas import tpu as pltpu

F32 = jnp.float32
BF16 = jnp.bfloat16

GRID_W = 64
HEAD_DIM = 128
WIN_ROWS = 8
WIN_COLS = 16
SHORT_CONV = 3
HYENA_ORDER = 2
FILTER_EMB = 33
DECAY_TARGET = 1e-2
FAST_DECAY_PCT = 0.3
SLOW_DECAY_PCT = 1.5
N_EXPERTS = 64
N_GROUPS = 8
TOPK_GROUPS = 4
TOP_K = 8
ROUTED_SCALE = 2.5
MOE_BLOCK = 256
NORM_EPS = 1e-6

LANES = 128
VMEM_LIMIT_BYTES = 48 << 20
MASK_VALUE = -1e30

ATT_ROWS = 8
ATT_HALO = 4
ATT_HEADS = 2
FFT_N2 = 256
FFT_CB = 8
NT_DIMS = (((1,), (1,)), ((), ()))


def _cparams(*sem):
    return pltpu.CompilerParams(dimension_semantics=sem, vmem_limit_bytes=VMEM_LIMIT_BYTES)


def _tile(n, t):
    t = min(n, t)
    assert n % t == 0, (n, t)
    return t


def _norm_modulate(x_ref, g_ref, sh_ref, sc_ref, out_ref, chunk=32):
    g = g_ref[...]
    sc1 = 1.0 + sc_ref[...]
    sh = sh_ref[...]
    rows = x_ref.shape[0]
    chunk = min(chunk, rows)

    def body(r, carry):
        sl = pl.ds(pl.multiple_of(r * chunk, chunk), chunk)
        x = x_ref[sl, :]
        ms = jnp.mean(x * x, axis=-1, keepdims=True)
        y = (x * lax.rsqrt(ms + NORM_EPS)) * g
        out_ref[sl, :] = (y * sc1 + sh).astype(out_ref.dtype)
        return carry

    lax.fori_loop(0, rows // chunk, body, 0)


def _qkv_kernel(x_ref, g_ref, sh_ref, sc_ref, w_ref, gain_ref, o_ref, hn_ref, *, n_norm_tiles):
    j = pl.program_id(1)

    @pl.when(j == 0)
    def _():
        _norm_modulate(x_ref, g_ref, sh_ref, sc_ref, hn_ref)

    y = jnp.dot(hn_ref[...], w_ref[...], preferred_element_type=F32)
    heads = [y[:, hh * HEAD_DIM:(hh + 1) * HEAD_DIM] for hh in range(o_ref.shape[0])]

    @pl.when(j < n_norm_tiles)
    def _():
        gain = gain_ref[...]
        for hh, yh in enumerate(heads):
            ms = jnp.mean(yh * yh, axis=-1, keepdims=True)
            o_ref[hh] = ((yh * lax.rsqrt(ms + NORM_EPS)) * gain).astype(o_ref.dtype)

    @pl.when(j >= n_norm_tiles)
    def _():
        for hh, yh in enumerate(heads):
            o_ref[hh] = yh.astype(o_ref.dtype)


def _qkv_proj(x, g, shift, scale, w_bf16, gains, tm=512, tn=512):
    S, D = x.shape
    N = w_bf16.shape[1]
    tm, tn = _tile(S, tm), _tile(D, tn)
    tiles_per_region = D // tn
    return pl.pallas_call(
        functools.partial(_qkv_kernel, n_norm_tiles=2 * tiles_per_region),
        grid=(S // tm, N // tn),
        in_specs=[
            pl.BlockSpec((tm, D), lambda i, j: (i, 0)),
            pl.BlockSpec((1, D), lambda i, j: (0, 0)),
            pl.BlockSpec((1, D), lambda i, j: (0, 0)),
            pl.BlockSpec((1, D), lambda i, j: (0, 0)),
            pl.BlockSpec((D, tn), lambda i, j: (0, j)),
            pl.BlockSpec((None, 1, HEAD_DIM), lambda i, j: (j // tiles_per_region, 0, 0)),
        ],
        out_specs=pl.BlockSpec((tn // HEAD_DIM, tm, HEAD_DIM), lambda i, j: (j, i, 0)),
        out_shape=jax.ShapeDtypeStruct((N // HEAD_DIM, S, HEAD_DIM), BF16),
        scratch_shapes=[pltpu.VMEM((tm, D), BF16)],
        compiler_params=_cparams("parallel", "arbitrary"),
        name="qkv_proj",
    )(x, g, shift, scale, w_bf16, gains)


def _attn_kernel(q_ref, kp_ref, kc_ref, kn_ref, vp_ref, vc_ref, vn_ref, kx_ref, vx_ref,
                 b_ref, o_ref):
    halo = ATT_HALO * GRID_W
    blk = ATT_ROWS * GRID_W
    bias_cols = (0, halo, halo + blk, 2 * halo + blk)
    for hh in range(q_ref.shape[0]):
        q = q_ref[hh]
        k_parts = (kp_ref[hh, blk - halo:, :], kc_ref[hh], kn_ref[hh, :halo, :], kx_ref[hh])
        v_parts = (vp_ref[hh, blk - halo:, :], vc_ref[hh], vn_ref[hh, :halo, :], vx_ref[hh])
        scores = []
        for idx, k in enumerate(k_parts):
            s = lax.dot_general(q, k, NT_DIMS, preferred_element_type=F32)
            if idx < 3:
                s = s + b_ref[hh, :, bias_cols[idx]:bias_cols[idx + 1]]
            scores.append(s)
        m = scores[0].max(axis=-1, keepdims=True)
        for s in scores[1:]:
            m = jnp.maximum(m, s.max(axis=-1, keepdims=True))
        den = jnp.zeros_like(m)
        acc = jnp.zeros((blk, HEAD_DIM), F32)
        for s, v in zip(scores, v_parts):
            p = jnp.exp(s - m)
            den = den + p.sum(axis=-1, keepdims=True)
            acc = acc + jnp.dot(p.astype(BF16), v, preferred_element_type=F32)
        o_ref[:, hh * HEAD_DIM:(hh + 1) * HEAD_DIM] = (acc / den).astype(o_ref.dtype)


def _attention_bias(rpb, rows):
    n_rb = rows // ATT_ROWS
    rb_rep = np.array([0, min(1, n_rb - 1), n_rb - 1])
    rl = np.arange(ATT_ROWS)
    i = np.arange(ATT_ROWS + 2 * ATT_HALO)
    r = rb_rep[:, None] * ATT_ROWS + rl[None, :]
    a = rb_rep[:, None] * ATT_ROWS - ATT_HALO + i[None, :]
    rs = np.clip(r - WIN_ROWS // 2, 0, rows - WIN_ROWS)
    row_ok = (a[:, None, :] >= rs[:, :, None]) & (a[:, None, :] < rs[:, :, None] + WIN_ROWS)
    drow = np.clip(a[:, None, :] - r[:, :, None] + (WIN_ROWS - 1), 0, 2 * WIN_ROWS - 2)
    col = np.arange(GRID_W)
    cs = np.clip(col - WIN_COLS // 2, 0, GRID_W - WIN_COLS)
    col_ok = (col[None, :] >= cs[:, None]) & (col[None, :] < cs[:, None] + WIN_COLS)
    dcol = np.clip(col[None, :] - col[:, None] + (WIN_COLS - 1), 0, 2 * WIN_COLS - 2)
    tcol = jnp.where(col_ok[None, None], rpb[:, :, dcol], MASK_VALUE)
    val = jnp.take(tcol, drow.reshape(-1), axis=1)
    H = rpb.shape[0]
    val = val.reshape((H,) + drow.shape + (GRID_W, GRID_W))
    val = jnp.where(row_ok[None, :, :, :, None, None], val, MASK_VALUE)
    val = jnp.transpose(val, (1, 0, 2, 4, 3, 5))
    return val.reshape(3, H, ATT_ROWS * GRID_W, (ATT_ROWS + 2 * ATT_HALO) * GRID_W).astype(F32)


def _attention(qkvh, ctxh, bias, S, D):
    H = D // HEAD_DIM
    C = ctxh.shape[1]
    blk = ATT_ROWS * GRID_W
    n_rb = S // blk
    nkeys = (ATT_ROWS + 2 * ATT_HALO) * GRID_W

    hp = _tile(H, ATT_HEADS)
    part = H // hp

    def spec(head_off, shift):
        return pl.BlockSpec(
            (hp, blk, HEAD_DIM),
            lambda h, rb: (head_off + h, jnp.clip(rb + shift, 0, n_rb - 1), 0))

    def variant(rb):
        return jnp.where(rb == 0, 0, jnp.where(rb == n_rb - 1, 2, 1))

    return pl.pallas_call(
        _attn_kernel,
        grid=(H // hp, n_rb),
        in_specs=[
            spec(0, 0),
            spec(part, -1), spec(part, 0), spec(part, 1),
            spec(2 * part, -1), spec(2 * part, 0), spec(2 * part, 1),
            pl.BlockSpec((hp, C, HEAD_DIM), lambda h, rb: (part + h, 0, 0)),
            pl.BlockSpec((hp, C, HEAD_DIM), lambda h, rb: (2 * part + h, 0, 0)),
            pl.BlockSpec((None, hp, blk, nkeys), lambda h, rb: (variant(rb), h, 0, 0)),
        ],
        out_specs=pl.BlockSpec((blk, hp * HEAD_DIM), lambda h, rb: (rb, h)),
        out_shape=jax.ShapeDtypeStruct((S, D), BF16),
        compiler_params=_cparams("parallel", "arbitrary"),
        name="nbr_attention",
    )(qkvh, qkvh, qkvh, qkvh, qkvh, qkvh, qkvh, ctxh, ctxh, bias)


def _mm_kernel(*refs, has_bias, has_resid):
    a_ref, w_ref = refs[0], refs[1]
    o_ref = refs[-1]
    y = jnp.dot(a_ref[...], w_ref[...].astype(BF16), preferred_element_type=F32)
    k = 2
    if has_bias:
        y = y + refs[k][...]
        k += 1
    if has_resid:
        y = refs[k][...] + refs[k + 1][...] * y
    o_ref[...] = y.astype(o_ref.dtype)


def _matmul(a, w, bias=None, resid=None, gate=None, out_dtype=F32, tm=512, tn=512, name="matmul",
            w_layer=None):
    M, K = a.shape
    N = w.shape[-1]
    tm, tn = _tile(M, tm), _tile(N, tn)
    if w_layer is None:
        w_spec = pl.BlockSpec((K, tn), lambda i, j: (0, j))
    else:
        w_spec = pl.BlockSpec((None, K, tn), lambda i, j: (w_layer, 0, j))
    in_specs = [pl.BlockSpec((tm, K), lambda i, j: (i, 0)), w_spec]
    args = [a, w]
    if bias is not None:
        in_specs.append(pl.BlockSpec((1, tn), lambda i, j: (0, j)))
        args.append(bias)
    if resid is not None:
        in_specs += [pl.BlockSpec((tm, tn), lambda i, j: (i, j)),
                     pl.BlockSpec((1, tn), lambda i, j: (0, j))]
        args += [resid, gate]
    return pl.pallas_call(
        functools.partial(_mm_kernel, has_bias=bias is not None, has_resid=resid is not None),
        grid=(M // tm, N // tn),
        in_specs=in_specs,
        out_specs=pl.BlockSpec((tm, tn), lambda i, j: (i, j)),
        out_shape=jax.ShapeDtypeStruct((M, N), out_dtype),
        compiler_params=_cparams("parallel", "arbitrary"),
        name=name,
    )(*args)


def _route_tile(logits, b_row):
    neg = -jnp.inf
    lane = lax.broadcasted_iota(jnp.int32, logits.shape, 1)
    lane_f = lane.astype(F32)
    grp = lane // (N_EXPERTS // N_GROUPS)
    scores = jax.nn.sigmoid(logits)
    choice = jnp.where(lane < N_EXPERTS, scores + b_row, neg)

    def first_max(c):
        m = jnp.max(c, axis=1, keepdims=True)
        idx = jnp.min(jnp.where(c == m, lane_f, float(LANES)), axis=1, keepdims=True)
        return m, idx

    grp_score = []
    for gi in range(N_GROUPS):
        cg = jnp.where(grp == gi, choice, neg)
        m1, i1 = first_max(cg)
        m2 = jnp.max(jnp.where(lane_f == i1, neg, cg), axis=1, keepdims=True)
        grp_score.append(m1 + m2)
    c = jnp.full(logits.shape, neg, F32)
    for gi in range(N_GROUPS):
        rank = jnp.zeros_like(grp_score[gi])
        for gj in range(N_GROUPS):
            if gj < gi:
                ahead = grp_score[gj] >= grp_score[gi]
            elif gj > gi:
                ahead = grp_score[gj] > grp_score[gi]
            else:
                continue
            rank = rank + jnp.where(ahead, 1.0, 0.0)
        c = jnp.where(grp == gi, jnp.where(rank < TOPK_GROUPS, choice, neg), c)

    ids = jnp.zeros(logits.shape, F32)
    wts = jnp.zeros(logits.shape, F32)
    total = jnp.zeros_like(grp_score[0])
    for k in range(TOP_K):
        _, idx = first_max(c)
        hit = lane_f == idx
        s = jnp.sum(jnp.where(hit, scores, 0.0), axis=1, keepdims=True)
        ids = jnp.where(lane == k, idx, ids)
        wts = jnp.where(lane == k, s, wts)
        total = total + s
        c = jnp.where(hit, neg, c)
    return ids, wts / total * ROUTED_SCALE


def _pack_halves(x):
    half = x.shape[1] // 2
    as_bits = lambda v: lax.bitcast_convert_type(v.astype(BF16).astype(F32), jnp.uint32)
    lo = lax.shift_right_logical(as_bits(x[:, :half]), jnp.uint32(16))
    hi = as_bits(x[:, half:]) & jnp.uint32(0xFFFF0000)
    return lo | hi


def _unpack_halves(u):
    lo = lax.bitcast_convert_type(lax.shift_left(u, jnp.uint32(16)), F32)
    hi = lax.bitcast_convert_type(u & jnp.uint32(0xFFFF0000), F32)
    return lo, hi


def _router_kernel(x_ref, g_ref, sh_ref, sc_ref, w_ref, b_ref, t_ref, e_ref, p_ref, hn_ref):
    _norm_modulate(x_ref, g_ref, sh_ref, sc_ref, hn_ref)
    hn = hn_ref[...]
    t_ref[...] = _pack_halves(hn)
    logits = jnp.dot(hn, w_ref[...], preferred_element_type=F32, precision=lax.Precision.HIGHEST)
    ids, wts = _route_tile(logits, b_ref[...])
    e_ref[...] = ids.astype(jnp.int32)
    p_ref[...] = wts


def _router(x, g, shift, scale, w_router_pad, b_router_pad, tm=256):
    T, D = x.shape
    NP = w_router_pad.shape[1]
    tm = _tile(T, tm)
    return pl.pallas_call(
        _router_kernel,
        grid=(T // tm,),
        in_specs=[
            pl.BlockSpec((tm, D), lambda i: (i, 0)),
            pl.BlockSpec((1, D), lambda i: (0, 0)),
            pl.BlockSpec((1, D), lambda i: (0, 0)),
            pl.BlockSpec((1, D), lambda i: (0, 0)),
            pl.BlockSpec((D, NP), lambda i: (0, 0)),
            pl.BlockSpec((1, NP), lambda i: (0, 0)),
        ],
        out_specs=[pl.BlockSpec((tm, D // 2), lambda i: (i, 0)),
                   pl.BlockSpec((tm, NP), lambda i: (i, 0)),
                   pl.BlockSpec((tm, NP), lambda i: (i, 0))],
        out_shape=[jax.ShapeDtypeStruct((T, D // 2), jnp.uint32),
                   jax.ShapeDtypeStruct((T, NP), jnp.int32),
                   jax.ShapeDtypeStruct((T, NP), F32)],
        scratch_shapes=[pltpu.VMEM((tm, D), F32)],
        compiler_params=_cparams("parallel"),
        name="moe_router",
    )(x, g, shift, scale, w_router_pad, b_router_pad)


def _row_gather_start(idx_ref, idx_base, n_rows, src_hbm, dst, sem):
    def body(r, carry):
        pltpu.make_async_copy(src_hbm.at[pl.ds(idx_ref[0, idx_base + r], 1), :],
                              dst.at[pl.ds(r, 1), :], sem).start()
        return carry

    lax.fori_loop(0, n_rows, body, 0, unroll=8)


def _gather_wait(dst, sem):
    pltpu.make_async_copy(dst, dst, sem).wait()


def _expert_kernel(be_ref, nused_ref, tok_ref, tok_next_ref, t_hbm, wg_ref, wu_ref, wd_ref,
                   y_ref, xbuf, sem):
    b = pl.program_id(0)
    n_used = nused_ref[0]
    slot = b % 2
    rows, half = xbuf.shape[1:]

    @pl.when((b == 0) & (n_used > 0))
    def _():
        _row_gather_start(tok_ref, 0, rows, t_hbm, xbuf.at[0], sem.at[0])

    @pl.when(b + 1 < n_used)
    def _():
        _row_gather_start(tok_next_ref, 0, rows, t_hbm, xbuf.at[1 - slot], sem.at[1 - slot])

    @pl.when(b < n_used)
    def _():
        _gather_wait(xbuf.at[slot], sem.at[slot])
        x_lo, x_hi = _unpack_halves(xbuf[slot])
        x_lo, x_hi = x_lo.astype(BF16), x_hi.astype(BF16)
        wg = wg_ref[...].astype(BF16)
        wu = wu_ref[...].astype(BF16)
        gt = (lax.dot_general(x_lo, wg[:, :half], NT_DIMS, preferred_element_type=F32)
              + lax.dot_general(x_hi, wg[:, half:], NT_DIMS, preferred_element_type=F32))
        up = (lax.dot_general(x_lo, wu[:, :half], NT_DIMS, preferred_element_type=F32)
              + lax.dot_general(x_hi, wu[:, half:], NT_DIMS, preferred_element_type=F32))
        h = (gt * jax.nn.sigmoid(gt)) * up
        y = jnp.dot(h.astype(BF16), wd_ref[...].astype(BF16), preferred_element_type=F32)
        y_ref[...] = _pack_halves(y)

    @pl.when(b >= n_used)
    def _():
        y_ref[...] = jnp.zeros(y_ref.shape, y_ref.dtype)


def _experts(t_packed, tok_buf, wg, wu, wd, layer, block_e, n_used):
    half = t_packed.shape[1]
    D = 2 * half
    Hd = wg.shape[-2]
    n_blocks = tok_buf.shape[0] // MOE_BLOCK
    tok3 = tok_buf.reshape(n_blocks, 1, MOE_BLOCK)
    once = pl.Buffered(1)
    smem_idx = lambda shift: pl.BlockSpec(
        (None, 1, MOE_BLOCK), lambda b, be, nu: (jnp.minimum(b + shift, n_blocks - 1), 0, 0),
        memory_space=pltpu.SMEM)
    wspec = lambda: pl.BlockSpec((None, None, Hd, D), lambda b, be, nu: (layer, be[b], 0, 0),
                                 pipeline_mode=once)
    return pl.pallas_call(
        _expert_kernel,
        grid_spec=pltpu.PrefetchScalarGridSpec(
            num_scalar_prefetch=2,
            grid=(n_blocks,),
            in_specs=[smem_idx(0), smem_idx(1), pl.BlockSpec(memory_space=pl.ANY),
                      wspec(), wspec(), wspec()],
            out_specs=pl.BlockSpec((MOE_BLOCK, half), lambda b, be, nu: (b, 0)),
            scratch_shapes=[pltpu.VMEM((2, MOE_BLOCK, half), jnp.uint32),
                            pltpu.SemaphoreType.DMA((2,))],
        ),
        out_shape=jax.ShapeDtypeStruct((n_blocks * MOE_BLOCK, half), jnp.uint32),
        compiler_params=_cparams("arbitrary"),
        name="moe_experts",
    )(block_e, n_used, tok3, tok3, t_packed, wg, wu, wd)


def _combine_kernel(pos_ref, pos_next_ref, x_ref, ysh_ref, w_ref, gate_ref, y_hbm, o_ref,
                    ybuf, sem):
    i = pl.program_id(0)
    slot = i % 2
    n_k, tm, half = ybuf.shape[1:]
    width = min(half, 2 * LANES)

    def start(idx_ref, s):
        for k in range(n_k):
            _row_gather_start(idx_ref, k * tm, tm, y_hbm, ybuf.at[s, k], sem.at[s])

    @pl.when(i == 0)
    def _():
        start(pos_ref, 0)

    @pl.when(i + 1 < pl.num_programs(0))
    def _():
        start(pos_next_ref, 1 - slot)

    _gather_wait(ybuf.at[slot], sem.at[slot])
    w = w_ref[...]
    wk = [w[:, k:k + 1] for k in range(n_k)]
    for s in range(half // width):
        c_lo, c_hi = pl.ds(s * width, width), pl.ds(half + s * width, width)
        acc_lo, acc_hi = _unpack_halves(ysh_ref[:, c_lo])
        for k in range(n_k):
            lo, hi = _unpack_halves(ybuf[slot, k, :, c_lo])
            acc_lo = acc_lo + wk[k] * lo
            acc_hi = acc_hi + wk[k] * hi
        o_ref[:, c_lo] = x_ref[:, c_lo] + gate_ref[:, c_lo] * acc_lo
        o_ref[:, c_hi] = x_ref[:, c_hi] + gate_ref[:, c_hi] * acc_hi


def _combine(x, ysh, y, pos, top_w, gate, tm=128):
    T, D = x.shape
    half = D // 2
    tm = _tile(T, tm)
    n_k = pos.shape[1]
    n_tiles = T // tm
    pos3 = jnp.transpose(pos.reshape(n_tiles, tm, n_k), (0, 2, 1)).reshape(n_tiles, 1, n_k * tm)
    smem_idx = lambda shift: pl.BlockSpec(
        (None, 1, n_k * tm), lambda i: (jnp.minimum(i + shift, n_tiles - 1), 0, 0),
        memory_space=pltpu.SMEM)
    return pl.pallas_call(
        _combine_kernel,
        grid=(n_tiles,),
        in_specs=[
            smem_idx(0), smem_idx(1),
            pl.BlockSpec((tm, D), lambda i: (i, 0)),
            pl.BlockSpec((tm, half), lambda i: (i, 0)),
            pl.BlockSpec((tm, n_k), lambda i: (i, 0)),
            pl.BlockSpec((1, D), lambda i: (0, 0)),
            pl.BlockSpec(memory_space=pl.ANY),
        ],
        out_specs=pl.BlockSpec((tm, D), lambda i: (i, 0)),
        out_shape=jax.ShapeDtypeStruct((T, D), F32),
        scratch_shapes=[pltpu.VMEM((2, n_k, tm, half), jnp.uint32),
                        pltpu.SemaphoreType.DMA((2,))],
        compiler_params=_cparams("arbitrary"),
        name="moe_combine",
    )(pos3, pos3, x, ysh, top_w, gate, y)


def _slot_kernel(eid_ref, pstart_ref, dest_ref, next_ref):
    n, _, ch = eid_ref.shape
    n_exp = pstart_ref.shape[0]
    next_ref[...] = pstart_ref[...]
    e_iota = lax.broadcasted_iota(jnp.int32, (n_exp, ch), 0)
    earlier = (lax.broadcasted_iota(jnp.int32, (ch, ch), 0)
               < lax.broadcasted_iota(jnp.int32, (ch, ch), 1)).astype(BF16)

    def body(r, carry):
        onehot = (eid_ref[r] == e_iota).astype(F32)
        before = jnp.dot(onehot.astype(BF16), earlier, preferred_element_type=F32)
        slot = jnp.sum(onehot * (next_ref[...] + before), axis=0, keepdims=True)
        dest_ref[r] = slot.astype(jnp.int32)
        next_ref[...] = next_ref[...] + jnp.sum(onehot, axis=1, keepdims=True)
        return carry

    lax.fori_loop(0, n, body, 0)


def _dispatch_plan(top_e, ch=256):
    T = top_e.shape[0]
    n_assign = T * TOP_K
    eid = top_e.reshape(-1).astype(jnp.int32)
    counts = jnp.sum((eid[:, None] == jnp.arange(N_EXPERTS)).astype(jnp.int32), axis=0)
    padded = (counts + MOE_BLOCK - 1) // MOE_BLOCK * MOE_BLOCK
    pend = jnp.cumsum(padded)
    pstart = pend - padded
    n_blocks = -(-n_assign // MOE_BLOCK) + N_EXPERTS
    n_rows = n_blocks * MOE_BLOCK
    ch = _tile(n_assign, ch)
    dest = pl.pallas_call(
        _slot_kernel,
        out_shape=jax.ShapeDtypeStruct((n_assign // ch, 1, ch), jnp.int32),
        scratch_shapes=[pltpu.VMEM((N_EXPERTS, 1), F32)],
        compiler_params=pltpu.CompilerParams(vmem_limit_bytes=VMEM_LIMIT_BYTES),
        name="moe_slots",
    )(eid.reshape(n_assign // ch, 1, ch), pstart.astype(F32)[:, None])
    dest = dest.reshape(-1)
    tok = jnp.repeat(jnp.arange(T, dtype=jnp.int32), TOP_K)
    tok_buf = jnp.zeros((n_rows,), jnp.int32).at[dest].set(tok)
    block_start = jnp.arange(n_blocks, dtype=jnp.int32) * MOE_BLOCK
    block_e = jnp.minimum(jnp.sum((pend[None, :] <= block_start[:, None]).astype(jnp.int32), axis=1),
                          N_EXPERTS - 1).astype(jnp.int32)
    n_used = (pend[-1] // MOE_BLOCK).astype(jnp.int32).reshape(1)
    return dest.reshape(T, TOP_K), tok_buf, block_e, n_used


def _moe_layer(x, g, shift, scale, gate, layer, w_router, b_router, w_gate, w_up, w_down,
               ws_gate, ws_up, ws_down):
    T, D = x.shape
    w_router_pad = jnp.zeros((D, LANES), F32).at[:, :N_EXPERTS].set(w_router)
    b_router_pad = jnp.zeros((1, LANES), F32).at[0, :N_EXPERTS].set(b_router)
    t_packed, top_e, top_w = _router(x, g, shift, scale, w_router_pad, b_router_pad)
    top_e, top_w = top_e[:, :TOP_K], top_w[:, :TOP_K]
    pos, tok_buf, block_e, n_used = _dispatch_plan(top_e)

    wg_t, wu_t = jnp.swapaxes(w_gate, -1, -2), jnp.swapaxes(w_up, -1, -2)
    wsg_t, wsu_t = jnp.swapaxes(ws_gate, -1, -2)[:, None], jnp.swapaxes(ws_up, -1, -2)[:, None]
    y = _experts(t_packed, tok_buf, wg_t, wu_t, w_down, layer, block_e, n_used)
    n_sh = T // MOE_BLOCK
    ysh = _experts(t_packed, jnp.arange(T, dtype=jnp.int32), wsg_t, wsu_t, ws_down[:, None], layer,
                   jnp.zeros((n_sh,), jnp.int32), jnp.full((1,), n_sh, jnp.int32))
    return _combine(x, ysh, y, pos, top_w, gate)


def _norm_mod_kernel(x_ref, g_ref, sh_ref, sc_ref, o_ref):
    _norm_modulate(x_ref, g_ref, sh_ref, sc_ref, o_ref)


def _norm_mod(x, g, shift, scale, tm=256):
    S, D = x.shape
    tm = _tile(S, tm)
    vec = pl.BlockSpec((1, D), lambda i: (0, 0))
    return pl.pallas_call(
        _norm_mod_kernel,
        grid=(S // tm,),
        in_specs=[pl.BlockSpec((tm, D), lambda i: (i, 0)), vec, vec, vec],
        out_specs=pl.BlockSpec((tm, D), lambda i: (i, 0)),
        out_shape=jax.ShapeDtypeStruct((S, D), BF16),
        compiler_params=_cparams("parallel"),
        name="norm_modulate",
    )(x, g, shift, scale)


def _inproj_t_kernel(wt_ref, hn_ref, b_ref, o_ref, *, groups):
    y = lax.dot_general(wt_ref[...], hn_ref[...], NT_DIMS, preferred_element_type=F32) + b_ref[...]
    per = y.shape[1] // FFT_N2
    i = pl.program_id(1)
    for s in range(groups):
        @pl.when(i % groups == s)
        def _():
            for q in range(per):
                o_ref[:, s * per + q, :] = y[:, q * FFT_N2:(q + 1) * FFT_N2]


def _inproj_t(hn, wt_bf16, bias_col, tm=512, tn=512):
    S, D = hn.shape
    N = wt_bf16.shape[0]
    tm, tn = _tile(S, tm), _tile(N, tn)
    n_chunks = S // FFT_N2
    per = tm // FFT_N2
    rows = min(8, n_chunks)
    groups = rows // per
    return pl.pallas_call(
        functools.partial(_inproj_t_kernel, groups=groups),
        grid=(N // tn, S // tm),
        in_specs=[
            pl.BlockSpec((tn, D), lambda j, i: (j, 0)),
            pl.BlockSpec((tm, D), lambda j, i: (i, 0)),
            pl.BlockSpec((tn, 1), lambda j, i: (j, 0)),
        ],
        out_specs=pl.BlockSpec((tn, rows, FFT_N2), lambda j, i: (j, i // groups, 0)),
        out_shape=jax.ShapeDtypeStruct((N, n_chunks, FFT_N2), F32),
        compiler_params=_cparams("parallel", "arbitrary"),
        name="hyena_inproj",
    )(wt_bf16, hn, bias_col)


def _filter_mlp_kernel(z_ref, w1_ref, b1_ref, w2_ref, b2_ref, w3_ref, b3_ref, fq_ref, o_ref):
    hp = lax.Precision.HIGHEST
    fq = fq_ref[...]
    h = jnp.sin(fq * (jnp.dot(z_ref[...], w1_ref[...], precision=hp,
                              preferred_element_type=F32) + b1_ref[...]))
    h = jnp.sin(fq * (jnp.dot(h, w2_ref[...], precision=hp,
                              preferred_element_type=F32) + b2_ref[...]))
    h = jnp.sin(fq * (jnp.dot(h, w3_ref[...], precision=hp,
                              preferred_element_type=F32) + b3_ref[...]))
    o_ref[...] = h.astype(o_ref.dtype)


def _filter_features(L, w1, b1, w2, b2, w3, b3, freq, tl=2048):
    Fh = w1.shape[1]
    t = jnp.linspace(0.0, 1.0, L, dtype=F32)[:, None]
    bands = (FILTER_EMB - 1) // 2
    f = jnp.linspace(1e-4, bands - 1, bands, dtype=F32)
    ang = (2.0 * math.pi / L) * jnp.arange(L, dtype=F32)[:, None] * f
    z = jnp.concatenate([t, jnp.cos(ang), -jnp.sin(ang)], axis=-1)
    zp = jnp.zeros((L, LANES), F32).at[:, :FILTER_EMB].set(z)
    w1p = jnp.zeros((LANES, Fh), F32).at[:FILTER_EMB].set(w1)
    tl = _tile(L, tl)
    row = lambda v: v.reshape(1, -1)
    const = lambda shape: pl.BlockSpec(shape, lambda i: (0, 0))
    feats = pl.pallas_call(
        _filter_mlp_kernel,
        grid=(L // tl,),
        in_specs=[pl.BlockSpec((tl, LANES), lambda i: (i, 0)),
                  const((LANES, Fh)), const((1, Fh)), const((Fh, Fh)), const((1, Fh)),
                  const((Fh, Fh)), const((1, Fh)), const((1, Fh))],
        out_specs=pl.BlockSpec((tl, Fh), lambda i: (i, 0)),
        out_shape=jax.ShapeDtypeStruct((L, Fh), BF16),
        compiler_params=_cparams("parallel"),
        name="hyena_filter_mlp",
    )(zp, w1p, row(b1), w2, row(b2), w3, row(b3), row(freq))
    zero = jnp.zeros((1, Fh), BF16)
    feats_all = jnp.concatenate([feats, zero, feats[:0:-1]], axis=0)
    t_all = jnp.concatenate([t[:, 0], t[:1, 0], t[:0:-1, 0]])[None, :]
    return feats_all, t_all


def _filter_taps_kernel(wt_ref, f_ref, t_ref, d_ref, o_ref):
    h = lax.dot_general(wt_ref[...], f_ref[...], NT_DIMS, preferred_element_type=F32)
    h = (h * jnp.exp(-(d_ref[...] * t_ref[...]))).astype(o_ref.dtype)
    for q in range(o_ref.shape[1]):
        o_ref[:, q, :] = h[:, q * FFT_N2:(q + 1) * FFT_N2]


def _filter_taps(wout_t, feats_all, t_all, deltas_col, tn=512):
    n_ord, _, D, Fh = wout_t.shape
    N = feats_all.shape[0]
    rows = 8
    tl = rows * FFT_N2
    assert (N // 2) % tl == 0, (N, tl)
    tn = _tile(D, tn)
    half = (N // 2) // tl
    return pl.pallas_call(
        _filter_taps_kernel,
        grid=(n_ord, D // tn, N // tl),
        in_specs=[
            pl.BlockSpec((None, None, tn, Fh), lambda n, c, l: (n, l // half, c, 0)),
            pl.BlockSpec((tl, Fh), lambda n, c, l: (l, 0)),
            pl.BlockSpec((1, tl), lambda n, c, l: (0, l)),
            pl.BlockSpec((tn, 1), lambda n, c, l: (c, 0)),
        ],
        out_specs=pl.BlockSpec((None, tn, rows, FFT_N2), lambda n, c, l: (n, c, l, 0)),
        out_shape=jax.ShapeDtypeStruct((n_ord, D, N // FFT_N2, FFT_N2), BF16),
        compiler_params=_cparams("parallel", "parallel", "arbitrary"),
        name="hyena_filter_taps",
    )(wout_t, feats_all, t_all, deltas_col)


def _dft_constants(N):
    N2 = FFT_N2
    N1 = N // N2
    n1 = np.arange(N1)
    th1 = 2.0 * np.pi * np.outer(n1, n1) / N1
    fl = np.concatenate([np.cos(th1), -np.sin(th1)], axis=0)
    tht = 2.0 * np.pi * np.outer(n1, np.arange(N2)) / N
    n2 = np.arange(N2)
    th2 = 2.0 * np.pi * np.outer(n2, n2) / N2
    fr, fi = np.cos(th2), -np.sin(th2)
    ilc = np.cos(th1) / N
    ils = -np.sin(th1) / N
    bf = lambda a: jnp.asarray(a, dtype=F32).astype(BF16)
    return dict(
        fl_full=bf(fl), fl_half=bf(fl[:, :N1 // 2]),
        tr=jnp.asarray(np.cos(tht), F32), ti=jnp.asarray(-np.sin(tht), F32),
        fr=bf(fr), fi=bf(fi), nfi=bf(-fi),
        ilc=bf(ilc[:N1 // 2]), ils=bf(ils[:N1 // 2]),
    )


def _dft_forward(z_of, n_ch, n1, fl_ref, tr_ref, ti_ref, fr_ref, fi_ref, nfi_ref, br_ref, bi_ref):
    tr, ti = tr_ref[...], ti_ref[...]
    for c in range(n_ch):
        a = jnp.dot(fl_ref[...], z_of(c), preferred_element_type=F32)
        ar, ai = a[:n1], a[n1:]
        br_ref[c * n1:(c + 1) * n1, :] = (ar * tr - ai * ti).astype(BF16)
        bi_ref[c * n1:(c + 1) * n1, :] = (ar * ti + ai * tr).astype(BF16)
    br, bi = br_ref[...], bi_ref[...]
    xr = (jnp.dot(br, fr_ref[...], preferred_element_type=F32)
          + jnp.dot(bi, nfi_ref[...], preferred_element_type=F32))
    xi = (jnp.dot(br, fi_ref[...], preferred_element_type=F32)
          + jnp.dot(bi, fr_ref[...], preferred_element_type=F32))
    return xr, xi


def _filter_fft_kernel(kk_ref, fl_ref, tr_ref, ti_ref, fr_ref, fi_ref, nfi_ref,
                       kr_ref, ki_ref, br_ref, bi_ref):
    n1 = tr_ref.shape[0]
    n_ch = kk_ref.shape[0] // n1

    def z_of(c):
        kk = kk_ref[c * n1:(c + 1) * n1, :]
        l1 = jnp.sum(jnp.abs(kk.astype(F32)), axis=-1, keepdims=True)
        l1 = jnp.sum(l1, axis=0, keepdims=True)
        return (kk.astype(F32) * (1.0 / l1)).astype(BF16)

    xr, xi = _dft_forward(z_of, n_ch, n1, fl_ref, tr_ref, ti_ref, fr_ref, fi_ref, nfi_ref,
                          br_ref, bi_ref)
    kr_ref[...] = xr.astype(kr_ref.dtype)
    ki_ref[...] = xi.astype(ki_ref.dtype)


def _filter_fft(kk, dc):
    n_ord, D, N1, _ = kk.shape
    cb = _tile(D, FFT_CB)
    rows = cb * N1
    kk2 = kk.reshape(n_ord, D * N1, FFT_N2)
    const = lambda a: pl.BlockSpec(a.shape, lambda n, c: (0, 0))
    consts = [dc['fl_full'], dc['tr'], dc['ti'], dc['fr'], dc['fi'], dc['nfi']]
    blk = pl.BlockSpec((None, rows, FFT_N2), lambda n, c: (n, c, 0))
    return pl.pallas_call(
        _filter_fft_kernel,
        grid=(n_ord, D // cb),
        in_specs=[blk] + [const(a) for a in consts],
        out_specs=[blk, blk],
        out_shape=[jax.ShapeDtypeStruct((n_ord, D * N1, FFT_N2), BF16)] * 2,
        scratch_shapes=[pltpu.VMEM((rows, FFT_N2), BF16)] * 2,
        compiler_params=_cparams("parallel", "parallel"),
        name="hyena_filter_fft",
    )(kk2, *consts)


def _short_conv(u, p, k, n1h):
    rows, n2 = u.shape
    lane = lax.broadcasted_iota(jnp.int32, u.shape, 1)
    r1 = lax.broadcasted_iota(jnp.int32, u.shape, 0) % n1h
    back = pltpu.roll(u, 1, axis=1)
    prev = jnp.where(lane == 0,
                     jnp.where(r1 == 0, 0.0, pltpu.roll(back, 1, axis=0)), back)
    fwd = pltpu.roll(u, n2 - 1, axis=1)
    nxt = jnp.where(lane == n2 - 1,
                    jnp.where(r1 == n1h - 1, 0.0, pltpu.roll(fwd, rows - 1, axis=0)), fwd)
    c = 4 * k
    return (p[:, c:c + 1] * prev + p[:, c + 1:c + 2] * u + p[:, c + 2:c + 3] * nxt
            + p[:, c + 3:c + 4])


def _hyena_conv_kernel(uv_ref, ug0_ref, ug1_ref, p_ref, kr_ref, ki_ref,
                       fl_ref, tr_ref, ti_ref, fr_ref, fi_ref, nfi_ref, ilc_ref, ils_ref,
                       o_ref, br_ref, bi_ref, y_ref):
    n1 = tr_ref.shape[0]
    n1h = n1 // 2
    n_ch = uv_ref.shape[0] // n1h
    p = p_ref[...]
    tr, ti = tr_ref[...], ti_ref[...]
    gate_refs = (ug0_ref, ug1_ref)

    z = _short_conv(uv_ref[...], p, 0, n1h)
    for n in range(HYENA_ORDER):
        zb = z.astype(BF16)
        xr, xi = _dft_forward(lambda c: zb[c * n1h:(c + 1) * n1h, :], n_ch, n1,
                              fl_ref, tr_ref, ti_ref, fr_ref, fi_ref, nfi_ref, br_ref, bi_ref)
        kr, ki = kr_ref[n].astype(F32), ki_ref[n].astype(F32)
        yr = (xr * kr - xi * ki).astype(BF16)
        yi = (xr * ki + xi * kr).astype(BF16)
        gr = (jnp.dot(yr, fr_ref[...], preferred_element_type=F32)
              + jnp.dot(yi, fi_ref[...], preferred_element_type=F32))
        gi = (jnp.dot(yi, fr_ref[...], preferred_element_type=F32)
              + jnp.dot(yr, nfi_ref[...], preferred_element_type=F32))
        for c in range(n_ch):
            grc, gic = gr[c * n1:(c + 1) * n1], gi[c * n1:(c + 1) * n1]
            hr = (grc * tr + gic * ti).astype(BF16)
            hi = (gic * tr - grc * ti).astype(BF16)
            y_ref[c * n1h:(c + 1) * n1h, :] = (
                jnp.dot(ilc_ref[...], hr, preferred_element_type=F32)
                + jnp.dot(ils_ref[...], hi, preferred_element_type=F32))
        gate = _short_conv(gate_refs[n][...], p, n + 1, n1h)
        z = gate * (y_ref[...] + p[:, 12 + n:13 + n] * z)
    o_ref[...] = z.astype(o_ref.dtype)


def _hyena_conv(ut, params_rows, kr, ki, dc, D, L):
    N1 = 2 * L // FFT_N2
    n1h = N1 // 2
    cb = _tile(D, FFT_CB)
    rows = cb * n1h
    u3 = ut.reshape(3, D * n1h, FFT_N2)
    ublk = lambda part: pl.BlockSpec((None, rows, FFT_N2), lambda c: (part, c, 0))
    const = lambda a: pl.BlockSpec(a.shape, lambda c: (0, 0))
    consts = [dc['fl_half'], dc['tr'], dc['ti'], dc['fr'], dc['fi'], dc['nfi'], dc['ilc'], dc['ils']]
    kblk = pl.BlockSpec((HYENA_ORDER, cb * N1, FFT_N2), lambda c: (0, c, 0))
    out = pl.pallas_call(
        _hyena_conv_kernel,
        grid=(D // cb,),
        in_specs=[ublk(0), ublk(1), ublk(2),
                  pl.BlockSpec((rows, 16), lambda c: (c, 0)), kblk, kblk]
                 + [const(a) for a in consts],
        out_specs=pl.BlockSpec((rows, FFT_N2), lambda c: (c, 0)),
        out_shape=jax.ShapeDtypeStruct((D * n1h, FFT_N2), BF16),
        scratch_shapes=[pltpu.VMEM((cb * N1, FFT_N2), BF16)] * 2
                       + [pltpu.VMEM((rows, FFT_N2), F32)],
        compiler_params=_cparams("parallel"),
        name="hyena_long_conv",
    )(u3, u3, u3, params_rows, kr, ki, *consts)
    return out.reshape(D, L)


def _hyena_layer(x, g, shift, scale, gate, w_in, b_in, conv_w, conv_b, f_w1, f_b1, f_w2, f_b2,
                 f_w3, f_b3, f_freq, f_wout, skip, w_out, b_out):
    L, D = x.shape
    n1h = L // FFT_N2
    hn = _norm_mod(x, g, shift, scale)
    ut = _inproj_t(hn, w_in.T.astype(BF16), b_in[:, None])

    feats_all, t_all = _filter_features(L, f_w1, f_b1, f_w2, f_b2, f_w3, f_b3, f_freq)
    deltas = jnp.abs(jnp.linspace(math.log(DECAY_TARGET) / SLOW_DECAY_PCT,
                                  math.log(DECAY_TARGET) / FAST_DECAY_PCT, D, dtype=F32))
    wout_t = jnp.transpose(f_wout, (1, 2, 3, 0)).astype(BF16)
    kk = _filter_taps(wout_t, feats_all, t_all, deltas[:, None])
    dc = _dft_constants(2 * L)
    kr, ki = _filter_fft(kk, dc)

    cw = conv_w.reshape(SHORT_CONV, HYENA_ORDER + 1, D)
    cb = conv_b.reshape(HYENA_ORDER + 1, D)
    cols = []
    for part in range(HYENA_ORDER + 1):
        cols += [cw[0, part], cw[1, part], cw[2, part], cb[part]]
    cols += [skip[0], skip[1], jnp.zeros((D,), F32), jnp.zeros((D,), F32)]
    params_rows = jnp.repeat(jnp.stack(cols, axis=1), n1h, axis=0)

    zt = _hyena_conv(ut, params_rows, kr, ki, dc, D, L)
    return _matmul(zt.T, w_out.astype(BF16), bias=b_out[None, :], resid=x, gate=gate,
                   name="hyena_outproj")


def _attention_layer(x, hc, g, shift, scale, gate, cshift, cscale, w_qkv, w_out, q_gain, k_gain, rpb):
    S, D = x.shape
    rows = S // GRID_W
    w_bf16 = w_qkv.astype(BF16)
    gains = jnp.stack([q_gain * (HEAD_DIM ** -0.5), k_gain, jnp.ones_like(k_gain)])[:, None, :]
    qkvh = _qkv_proj(x, g, shift, scale, w_bf16, gains)
    ctxh = _qkv_proj(hc, g, cshift, cscale, w_bf16, gains)
    bias = _attention_bias(rpb, rows)
    o = _attention(qkvh, ctxh, bias, S, D)
    return _matmul(o, w_out.astype(BF16), resid=x, gate=gate, name="attn_outproj")


def _ada_rows(sc, scc, ada_w, ada_b, layer):
    D = sc.shape[-1]
    a = jnp.zeros((8, D), F32).at[0].set(sc[0]).at[1].set(scc).astype(BF16)
    return _matmul(a, ada_w, bias=ada_b[layer][None, :], tm=8, tn=512, name="adaln", w_layer=layer)


def kernel(x, c, ctx, c_ctx, ada_w, ada_b, norm_mix_g, norm_ffn_g, na_w_qkv, na_w_out, na_q_gain, na_k_gain, na_rpb, hy_w_in, hy_b_in, hy_conv_w, hy_conv_b, hy_f_w1, hy_f_b1, hy_f_w2, hy_f_b2, hy_f_w3, hy_f_b3, hy_f_freq, hy_f_wout, hy_skip, hy_w_out, hy_b_out, moe_w_router, moe_b_router, moe_w_gate, moe_w_up, moe_w_down, moe_ws_gate, moe_ws_up, moe_ws_down):
    B, S, D = x.shape
    assert B == 1 and D % HEAD_DIM == 0 and S % (ATT_ROWS * GRID_W) == 0
    depth = ada_w.shape[0]
    sc = jax.nn.silu(c)
    scc = jax.nn.silu(c_ctx)
    xs = x[0]
    hc = ctx[0]
    row = lambda v: v.reshape(1, D)
    for i in range(depth):
        j = i // 2
        ada = _ada_rows(sc, scc, ada_w, ada_b, i)
        sh_a, sc_a, g_a, sh_f, sc_f, g_f = [row(v) for v in jnp.split(ada[0], 6)]
        if i % 2 == 0:
            csh_a, csc_a = [row(v) for v in jnp.split(ada[1], 6)[:2]]
            xs = _attention_layer(xs, hc, row(norm_mix_g[i]), sh_a, sc_a, g_a, csh_a, csc_a,
                                  na_w_qkv[j], na_w_out[j], na_q_gain[j], na_k_gain[j],
                                  na_rpb[j])
        else:
            xs = _hyena_layer(xs, row(norm_mix_g[i]), sh_a, sc_a, g_a,
                              hy_w_in[j], hy_b_in[j], hy_conv_w[j], hy_conv_b[j],
                              hy_f_w1[j], hy_f_b1[j], hy_f_w2[j], hy_f_b2[j], hy_f_w3[j], hy_f_b3[j],
                              hy_f_freq[j], hy_f_wout[j], hy_skip[j], hy_w_out[j], hy_b_out[j])
        xs = _moe_layer(xs, row(norm_ffn_g[i]), sh_f, sc_f, g_f, i,
                        moe_w_router[i], moe_b_router[i], moe_w_gate, moe_w_up, moe_w_down,
                        moe_ws_gate, moe_ws_up, moe_ws_down)
    return xs[None]
```

```python
import functools
import math

import numpy as np
import jax
import jax.numpy as jnp
from jax import lax
from jax.experimental import pallas as pl
from jax.experimental.pallas import tpu as pltpu

F32 = jnp.float32
BF16 = jnp.bfloat16

GRID_W = 64
HEAD_DIM = 128
WIN_ROWS = 8
WIN_COLS = 16
SHORT_CONV = 3
HYENA_ORDER = 2
FILTER_EMB = 33
DECAY_TARGET = 1e-2
FAST_DECAY_PCT = 0.3
SLOW_DECAY_PCT = 1.5
N_EXPERTS = 64
N_GROUPS = 8
TOPK_GROUPS = 4
TOP_K = 8
ROUTED_SCALE = 2.5
MOE_BLOCK = 256
NORM_EPS = 1e-6

LANES = 128
VMEM_LIMIT_BYTES = 48 << 20
MASK_VALUE = -1e30

ATT_ROWS = 8
ATT_HALO = 4
ATT_HEADS = 2
FFT_N2 = 256
FFT_CB = 8
NT_DIMS = (((1,), (1,)), ((), ()))


def _cparams(*sem):
    return pltpu.CompilerParams(dimension_semantics=sem, vmem_limit_bytes=VMEM_LIMIT_BYTES)


def _tile(n, t):
    t = min(n, t)
    assert n % t == 0, (n, t)
    return t


def _norm_modulate(x_ref, g_ref, sh_ref, sc_ref, out_ref, chunk=32):
    g = g_ref[...]
    sc1 = 1.0 + sc_ref[...]
    sh = sh_ref[...]
    rows = x_ref.shape[0]
    chunk = min(chunk, rows)

    def body(r, carry):
        sl = pl.ds(pl.multiple_of(r * chunk, chunk), chunk)
        x = x_ref[sl, :]
        ms = jnp.mean(x * x, axis=-1, keepdims=True)
        y = (x * lax.rsqrt(ms + NORM_EPS)) * g
        out_ref[sl, :] = (y * sc1 + sh).astype(out_ref.dtype)
        return carry

    lax.fori_loop(0, rows // chunk, body, 0)


def _qkv_kernel(x_ref, g_ref, sh_ref, sc_ref, w_ref, gain_ref, o_ref, hn_ref, *, n_norm_tiles):
    j = pl.program_id(1)

    @pl.when(j == 0)
    def _():
        _norm_modulate(x_ref, g_ref, sh_ref, sc_ref, hn_ref)

    y = jnp.dot(hn_ref[...], w_ref[...], preferred_element_type=F32)
    heads = [y[:, hh * HEAD_DIM:(hh + 1) * HEAD_DIM] for hh in range(o_ref.shape[0])]

    @pl.when(j < n_norm_tiles)
    def _():
        gain = gain_ref[...]
        for hh, yh in enumerate(heads):
            ms = jnp.mean(yh * yh, axis=-1, keepdims=True)
            o_ref[hh] = ((yh * lax.rsqrt(ms + NORM_EPS)) * gain).astype(o_ref.dtype)

    @pl.when(j >= n_norm_tiles)
    def _():
        for hh, yh in enumerate(heads):
            o_ref[hh] = yh.astype(o_ref.dtype)


def _qkv_proj(x, g, shift, scale, w_bf16, gains, tm=512, tn=512):
    S, D = x.shape
    N = w_bf16.shape[1]
    tm, tn = _tile(S, tm), _tile(D, tn)
    tiles_per_region = D // tn
    return pl.pallas_call(
        functools.partial(_qkv_kernel, n_norm_tiles=2 * tiles_per_region),
        grid=(S // tm, N // tn),
        in_specs=[
            pl.BlockSpec((tm, D), lambda i, j: (i, 0)),
            pl.BlockSpec((1, D), lambda i, j: (0, 0)),
            pl.BlockSpec((1, D), lambda i, j: (0, 0)),
            pl.BlockSpec((1, D), lambda i, j: (0, 0)),
            pl.BlockSpec((D, tn), lambda i, j: (0, j)),
            pl.BlockSpec((None, 1, HEAD_DIM), lambda i, j: (j // tiles_per_region, 0, 0)),
        ],
        out_specs=pl.BlockSpec((tn // HEAD_DIM, tm, HEAD_DIM), lambda i, j: (j, i, 0)),
        out_shape=jax.ShapeDtypeStruct((N // HEAD_DIM, S, HEAD_DIM), BF16),
        scratch_shapes=[pltpu.VMEM((tm, D), BF16)],
        compiler_params=_cparams("parallel", "arbitrary"),
        name="qkv_proj",
    )(x, g, shift, scale, w_bf16, gains)


def _attn_kernel(q_ref, kp_ref, kc_ref, kn_ref, vp_ref, vc_ref, vn_ref, kx_ref, vx_ref,
                 b_ref, o_ref):
    halo = ATT_HALO * GRID_W
    blk = ATT_ROWS * GRID_W
    bias_cols = (0, halo, halo + blk, 2 * halo + blk)
    for hh in range(q_ref.shape[0]):
        q = q_ref[hh]
        k_parts = (kp_ref[hh, blk - halo:, :], kc_ref[hh], kn_ref[hh, :halo, :], kx_ref[hh])
        v_parts = (vp_ref[hh, blk - halo:, :], vc_ref[hh], vn_ref[hh, :halo, :], vx_ref[hh])
        scores = []
        for idx, k in enumerate(k_parts):
            s = lax.dot_general(q, k, NT_DIMS, preferred_element_type=F32)
            if idx < 3:
                s = s + b_ref[hh, :, bias_cols[idx]:bias_cols[idx + 1]]
            scores.append(s)
        m = scores[0].max(axis=-1, keepdims=True)
        for s in scores[1:]:
            m = jnp.maximum(m, s.max(axis=-1, keepdims=True))
        den = jnp.zeros_like(m)
        acc = jnp.zeros((blk, HEAD_DIM), F32)
        for s, v in zip(scores, v_parts):
            p = jnp.exp(s - m)
            den = den + p.sum(axis=-1, keepdims=True)
            acc = acc + jnp.dot(p.astype(BF16), v, preferred_element_type=F32)
        o_ref[:, hh * HEAD_DIM:(hh + 1) * HEAD_DIM] = (acc / den).astype(o_ref.dtype)


def _attention_bias(rpb, rows):
    n_rb = rows // ATT_ROWS
    rb_rep = np.array([0, min(1, n_rb - 1), n_rb - 1])
    rl = np.arange(ATT_ROWS)
    i = np.arange(ATT_ROWS + 2 * ATT_HALO)
    r = rb_rep[:, None] * ATT_ROWS + rl[None, :]
    a = rb_rep[:, None] * ATT_ROWS - ATT_HALO + i[None, :]
    rs = np.clip(r - WIN_ROWS // 2, 0, rows - WIN_ROWS)
    row_ok = (a[:, None, :] >= rs[:, :, None]) & (a[:, None, :] < rs[:, :, None] + WIN_ROWS)
    drow = np.clip(a[:, None, :] - r[:, :, None] + (WIN_ROWS - 1), 0, 2 * WIN_ROWS - 2)
    col = np.arange(GRID_W)
    cs = np.clip(col - WIN_COLS // 2, 0, GRID_W - WIN_COLS)
    col_ok = (col[None, :] >= cs[:, None]) & (col[None, :] < cs[:, None] + WIN_COLS)
    dcol = np.clip(col[None, :] - col[:, None] + (WIN_COLS - 1), 0, 2 * WIN_COLS - 2)
    tcol = jnp.where(col_ok[None, None], rpb[:, :, dcol], MASK_VALUE)
    val = jnp.take(tcol, drow.reshape(-1), axis=1)
    H = rpb.shape[0]
    val = val.reshape((H,) + drow.shape + (GRID_W, GRID_W))
    val = jnp.where(row_ok[None, :, :, :, None, None], val, MASK_VALUE)
    val = jnp.transpose(val, (1, 0, 2, 4, 3, 5))
    return val.reshape(3, H, ATT_ROWS * GRID_W, (ATT_ROWS + 2 * ATT_HALO) * GRID_W).astype(F32)


def _attention(qkvh, ctxh, bias, S, D):
    H = D // HEAD_DIM
    C = ctxh.shape[1]
    blk = ATT_ROWS * GRID_W
    n_rb = S // blk
    nkeys = (ATT_ROWS + 2 * ATT_HALO) * GRID_W

    hp = _tile(H, ATT_HEADS)
    part = H // hp

    def spec(head_off, shift):
        return pl.BlockSpec(
            (hp, blk, HEAD_DIM),
            lambda h, rb: (head_off + h, jnp.clip(rb + shift, 0, n_rb - 1), 0))

    def variant(rb):
        return jnp.where(rb == 0, 0, jnp.where(rb == n_rb - 1, 2, 1))

    return pl.pallas_call(
        _attn_kernel,
        grid=(H // hp, n_rb),
        in_specs=[
            spec(0, 0),
            spec(part, -1), spec(part, 0), spec(part, 1),
            spec(2 * part, -1), spec(2 * part, 0), spec(2 * part, 1),
            pl.BlockSpec((hp, C, HEAD_DIM), lambda h, rb: (part + h, 0, 0)),
            pl.BlockSpec((hp, C, HEAD_DIM), lambda h, rb: (2 * part + h, 0, 0)),
            pl.BlockSpec((None, hp, blk, nkeys), lambda h, rb: (variant(rb), h, 0, 0)),
        ],
        out_specs=pl.BlockSpec((blk, hp * HEAD_DIM), lambda h, rb: (rb, h)),
        out_shape=jax.ShapeDtypeStruct((S, D), BF16),
        compiler_params=_cparams("parallel", "arbitrary"),
        name="nbr_attention",
    )(qkvh, qkvh, qkvh, qkvh, qkvh, qkvh, qkvh, ctxh, ctxh, bias)


def _mm_kernel(*refs, has_bias, has_resid):
    a_ref, w_ref = refs[0], refs[1]
    o_ref = refs[-1]
    y = jnp.dot(a_ref[...], w_ref[...].astype(BF16), preferred_element_type=F32)
    k = 2
    if has_bias:
        y = y + refs[k][...]
        k += 1
    if has_resid:
        y = refs[k][...] + refs[k + 1][...] * y
    o_ref[...] = y.astype(o_ref.dtype)


def _matmul(a, w, bias=None, resid=None, gate=None, out_dtype=F32, tm=512, tn=512, name="matmul",
            w_layer=None):
    M, K = a.shape
    N = w.shape[-1]
    tm, tn = _tile(M, tm), _tile(N, tn)
    if w_layer is None:
        w_spec = pl.BlockSpec((K, tn), lambda i, j: (0, j))
    else:
        w_spec = pl.BlockSpec((None, K, tn), lambda i, j: (w_layer, 0, j))
    in_specs = [pl.BlockSpec((tm, K), lambda i, j: (i, 0)), w_spec]
    args = [a, w]
    if bias is not None:
        in_specs.append(pl.BlockSpec((1, tn), lambda i, j: (0, j)))
        args.append(bias)
    if resid is not None:
        in_specs += [pl.BlockSpec((tm, tn), lambda i, j: (i, j)),
                     pl.BlockSpec((1, tn), lambda i, j: (0, j))]
        args += [resid, gate]
    return pl.pallas_call(
        functools.partial(_mm_kernel, has_bias=bias is not None, has_resid=resid is not None),
        grid=(M // tm, N // tn),
        in_specs=in_specs,
        out_specs=pl.BlockSpec((tm, tn), lambda i, j: (i, j)),
        out_shape=jax.ShapeDtypeStruct((M, N), out_dtype),
        compiler_params=_cparams("parallel", "arbitrary"),
        name=name,
    )(*args)


def _route_tile(logits, b_row):
    neg = -jnp.inf
    lane = lax.broadcasted_iota(jnp.int32, logits.shape, 1)
    lane_f = lane.astype(F32)
    grp = lane // (N_EXPERTS // N_GROUPS)
    scores = jax.nn.sigmoid(logits)
    choice = jnp.where(lane < N_EXPERTS, scores + b_row, neg)

    def first_max(c):
        m = jnp.max(c, axis=1, keepdims=True)
        idx = jnp.min(jnp.where(c == m, lane_f, float(LANES)), axis=1, keepdims=True)
        return m, idx

    grp_score = []
    for gi in range(N_GROUPS):
        cg = jnp.where(grp == gi, choice, neg)
        m1, i1 = first_max(cg)
        m2 = jnp.max(jnp.where(lane_f == i1, neg, cg), axis=1, keepdims=True)
        grp_score.append(m1 + m2)
    c = jnp.full(logits.shape, neg, F32)
    for gi in range(N_GROUPS):
        rank = jnp.zeros_like(grp_score[gi])
        for gj in range(N_GROUPS):
            if gj < gi:
                ahead = grp_score[gj] >= grp_score[gi]
            elif gj > gi:
                ahead = grp_score[gj] > grp_score[gi]
            else:
                continue
            rank = rank + jnp.where(ahead, 1.0, 0.0)
        c = jnp.where(grp == gi, jnp.where(rank < TOPK_GROUPS, choice, neg), c)

    ids = jnp.zeros(logits.shape, F32)
    wts = jnp.zeros(logits.shape, F32)
    total = jnp.zeros_like(grp_score[0])
    for k in range(TOP_K):
        _, idx = first_max(c)
        hit = lane_f == idx
        s = jnp.sum(jnp.where(hit, scores, 0.0), axis=1, keepdims=True)
        ids = jnp.where(lane == k, idx, ids)
        wts = jnp.where(lane == k, s, wts)
        total = total + s
        c = jnp.where(hit, neg, c)
    return ids, wts / total * ROUTED_SCALE


def _pack_halves(x):
    half = x.shape[1] // 2
    as_bits = lambda v: lax.bitcast_convert_type(v.astype(BF16).astype(F32), jnp.uint32)
    lo = lax.shift_right_logical(as_bits(x[:, :half]), jnp.uint32(16))
    hi = as_bits(x[:, half:]) & jnp.uint32(0xFFFF0000)
    return lo | hi


def _unpack_halves(u):
    lo = lax.bitcast_convert_type(lax.shift_left(u, jnp.uint32(16)), F32)
    hi = lax.bitcast_convert_type(u & jnp.uint32(0xFFFF0000), F32)
    return lo, hi


def _router_kernel(x_ref, g_ref, sh_ref, sc_ref, w_ref, b_ref, t_ref, e_ref, p_ref, hn_ref):
    _norm_modulate(x_ref, g_ref, sh_ref, sc_ref, hn_ref)
    hn = hn_ref[...]
    t_ref[...] = _pack_halves(hn)
    logits = jnp.dot(hn, w_ref[...], preferred_element_type=F32, precision=lax.Precision.HIGHEST)
    ids, wts = _route_tile(logits, b_ref[...])
    e_ref[...] = ids.astype(jnp.int32)
    p_ref[...] = wts


def _router(x, g, shift, scale, w_router_pad, b_router_pad, tm=256):
    T, D = x.shape
    NP = w_router_pad.shape[1]
    tm = _tile(T, tm)
    return pl.pallas_call(
        _router_kernel,
        grid=(T // tm,),
        in_specs=[
            pl.BlockSpec((tm, D), lambda i: (i, 0)),
            pl.BlockSpec((1, D), lambda i: (0, 0)),
            pl.BlockSpec((1, D), lambda i: (0, 0)),
            pl.BlockSpec((1, D), lambda i: (0, 0)),
            pl.BlockSpec((D, NP), lambda i: (0, 0)),
            pl.BlockSpec((1, NP), lambda i: (0, 0)),
        ],
        out_specs=[pl.BlockSpec((tm, D // 2), lambda i: (i, 0)),
                   pl.BlockSpec((tm, NP), lambda i: (i, 0)),
                   pl.BlockSpec((tm, NP), lambda i: (i, 0))],
        out_shape=[jax.ShapeDtypeStruct((T, D // 2), jnp.uint32),
                   jax.ShapeDtypeStruct((T, NP), jnp.int32),
                   jax.ShapeDtypeStruct((T, NP), F32)],
        scratch_shapes=[pltpu.VMEM((tm, D), F32)],
        compiler_params=_cparams("parallel"),
        name="moe_router",
    )(x, g, shift, scale, w_router_pad, b_router_pad)


def _row_gather_start(idx_ref, idx_base, n_rows, src_hbm, dst, sem):
    for r in range(n_rows):
        pltpu.make_async_copy(src_hbm.at[pl.ds(idx_ref[0, idx_base + r], 1), :],
                              dst.at[pl.ds(r, 1), :], sem).start()


def _gather_wait(dst, sem):
    pltpu.make_async_copy(dst, dst, sem).wait()


def _expert_kernel(be_ref, nused_ref, tok_ref, tok_next_ref, t_hbm, wg_ref, wu_ref, wd_ref,
                   y_ref, xbuf, sem, wbf):
    b = pl.program_id(0)
    n_used = nused_ref[0]
    slot = b % 2
    rows, half = xbuf.shape[1:]

    @pl.when((b == 0) | (be_ref[b] != be_ref[jnp.maximum(b - 1, 0)]))
    def _():
        wbf[0] = wg_ref[...].astype(BF16)
        wbf[1] = wu_ref[...].astype(BF16)
        wbf[2] = wd_ref[...].astype(BF16)

    @pl.when((b == 0) & (n_used > 0))
    def _():
        _row_gather_start(tok_ref, 0, rows, t_hbm, xbuf.at[0], sem.at[0])

    @pl.when(b + 1 < n_used)
    def _():
        _row_gather_start(tok_next_ref, 0, rows, t_hbm, xbuf.at[1 - slot], sem.at[1 - slot])

    @pl.when(b < n_used)
    def _():
        _gather_wait(xbuf.at[slot], sem.at[slot])
        x_lo, x_hi = _unpack_halves(xbuf[slot])
        x_lo, x_hi = x_lo.astype(BF16), x_hi.astype(BF16)
        gt = (lax.dot_general(x_lo, wbf[0, :, :half], NT_DIMS, preferred_element_type=F32)
              + lax.dot_general(x_hi, wbf[0, :, half:], NT_DIMS, preferred_element_type=F32))
        up = (lax.dot_general(x_lo, wbf[1, :, :half], NT_DIMS, preferred_element_type=F32)
              + lax.dot_general(x_hi, wbf[1, :, half:], NT_DIMS, preferred_element_type=F32))
        h = (gt * jax.nn.sigmoid(gt)) * up
        y = jnp.dot(h.astype(BF16), wbf[2], preferred_element_type=F32)
        y_ref[...] = _pack_halves(y)

    @pl.when(b >= n_used)
    def _():
        y_ref[...] = jnp.zeros(y_ref.shape, y_ref.dtype)


def _experts(t_packed, tok_buf, wg, wu, wd, layer, block_e, n_used):
    half = t_packed.shape[1]
    D = 2 * half
    Hd = wg.shape[-2]
    n_blocks = tok_buf.shape[0] // MOE_BLOCK
    tok3 = tok_buf.reshape(n_blocks, 1, MOE_BLOCK)
    once = pl.Buffered(1)
    smem_idx = lambda shift: pl.BlockSpec(
        (None, 1, MOE_BLOCK), lambda b, be, nu: (jnp.minimum(b + shift, n_blocks - 1), 0, 0),
        memory_space=pltpu.SMEM)
    wspec = lambda: pl.BlockSpec((None, None, Hd, D), lambda b, be, nu: (layer, be[b], 0, 0),
                                 pipeline_mode=once)
    return pl.pallas_call(
        _expert_kernel,
        grid_spec=pltpu.PrefetchScalarGridSpec(
            num_scalar_prefetch=2,
            grid=(n_blocks,),
            in_specs=[smem_idx(0), smem_idx(1), pl.BlockSpec(memory_space=pl.ANY),
                      wspec(), wspec(), wspec()],
            out_specs=pl.BlockSpec((MOE_BLOCK, half), lambda b, be, nu: (b, 0)),
            scratch_shapes=[pltpu.VMEM((2, MOE_BLOCK, half), jnp.uint32),
                            pltpu.SemaphoreType.DMA((2,)),
                            pltpu.VMEM((3, Hd, D), BF16)],
        ),
        out_shape=jax.ShapeDtypeStruct((n_blocks * MOE_BLOCK, half), jnp.uint32),
        compiler_params=_cparams("arbitrary"),
        name="moe_experts",
    )(block_e, n_used, tok3, tok3, t_packed, wg, wu, wd)


def _combine_kernel(pos_ref, pos_next_ref, x_ref, ysh_ref, w_ref, gate_ref, y_hbm, o_ref,
                    ybuf, sem):
    i = pl.program_id(0)
    slot = i % 2
    n_k, tm, half = ybuf.shape[1:]
    width = min(half, 2 * LANES)

    def start(idx_ref, s):
        for k in range(n_k):
            _row_gather_start(idx_ref, k * tm, tm, y_hbm, ybuf.at[s, k], sem.at[s])

    @pl.when(i == 0)
    def _():
        start(pos_ref, 0)

    @pl.when(i + 1 < pl.num_programs(0))
    def _():
        start(pos_next_ref, 1 - slot)

    _gather_wait(ybuf.at[slot], sem.at[slot])
    w = w_ref[...]
    wk = [w[:, k:k + 1] for k in range(n_k)]
    for s in range(half // width):
        c_lo, c_hi = pl.ds(s * width, width), pl.ds(half + s * width, width)
        acc_lo, acc_hi = _unpack_halves(ysh_ref[:, c_lo])
        for k in range(n_k):
            lo, hi = _unpack_halves(ybuf[slot, k, :, c_lo])
            acc_lo = acc_lo + wk[k] * lo
            acc_hi = acc_hi + wk[k] * hi
        o_ref[:, c_lo] = x_ref[:, c_lo] + gate_ref[:, c_lo] * acc_lo
        o_ref[:, c_hi] = x_ref[:, c_hi] + gate_ref[:, c_hi] * acc_hi


def _combine(x, ysh, y, pos, top_w, gate, tm=128):
    T, D = x.shape
    half = D // 2
    tm = _tile(T, tm)
    n_k = pos.shape[1]
    n_tiles = T // tm
    pos3 = jnp.transpose(pos.reshape(n_tiles, tm, n_k), (0, 2, 1)).reshape(n_tiles, 1, n_k * tm)
    smem_idx = lambda shift: pl.BlockSpec(
        (None, 1, n_k * tm), lambda i: (jnp.minimum(i + shift, n_tiles - 1), 0, 0),
        memory_space=pltpu.SMEM)
    return pl.pallas_call(
        _combine_kernel,
        grid=(n_tiles,),
        in_specs=[
            smem_idx(0), smem_idx(1),
            pl.BlockSpec((tm, D), lambda i: (i, 0)),
            pl.BlockSpec((tm, half), lambda i: (i, 0)),
            pl.BlockSpec((tm, n_k), lambda i: (i, 0)),
            pl.BlockSpec((1, D), lambda i: (0, 0)),
            pl.BlockSpec(memory_space=pl.ANY),
        ],
        out_specs=pl.BlockSpec((tm, D), lambda i: (i, 0)),
        out_shape=jax.ShapeDtypeStruct((T, D), F32),
        scratch_shapes=[pltpu.VMEM((2, n_k, tm, half), jnp.uint32),
                        pltpu.SemaphoreType.DMA((2,))],
        compiler_params=_cparams("arbitrary"),
        name="moe_combine",
    )(pos3, pos3, x, ysh, top_w, gate, y)


def _slot_kernel(eid_ref, pstart_ref, dest_ref, next_ref):
    n, _, ch = eid_ref.shape
    n_exp = pstart_ref.shape[0]
    next_ref[...] = pstart_ref[...]
    e_iota = lax.broadcasted_iota(jnp.int32, (n_exp, ch), 0)
    earlier = (lax.broadcasted_iota(jnp.int32, (ch, ch), 0)
               < lax.broadcasted_iota(jnp.int32, (ch, ch), 1)).astype(BF16)

    def body(r, carry):
        onehot = (eid_ref[r] == e_iota).astype(F32)
        before = jnp.dot(onehot.astype(BF16), earlier, preferred_element_type=F32)
        slot = jnp.sum(onehot * (next_ref[...] + before), axis=0, keepdims=True)
        dest_ref[r] = slot.astype(jnp.int32)
        next_ref[...] = next_ref[...] + jnp.sum(onehot, axis=1, keepdims=True)
        return carry

    lax.fori_loop(0, n, body, 0)


def _dispatch_plan(top_e, ch=256):
    T = top_e.shape[0]
    n_assign = T * TOP_K
    eid = top_e.reshape(-1).astype(jnp.int32)
    counts = jnp.sum((eid[:, None] == jnp.arange(N_EXPERTS)).astype(jnp.int32), axis=0)
    padded = (counts + MOE_BLOCK - 1) // MOE_BLOCK * MOE_BLOCK
    pend = jnp.cumsum(padded)
    pstart = pend - padded
    n_blocks = -(-n_assign // MOE_BLOCK) + N_EXPERTS
    n_rows = n_blocks * MOE_BLOCK
    ch = _tile(n_assign, ch)
    dest = pl.pallas_call(
        _slot_kernel,
        out_shape=jax.ShapeDtypeStruct((n_assign // ch, 1, ch), jnp.int32),
        scratch_shapes=[pltpu.VMEM((N_EXPERTS, 1), F32)],
        compiler_params=pltpu.CompilerParams(vmem_limit_bytes=VMEM_LIMIT_BYTES),
        name="moe_slots",
    )(eid.reshape(n_assign // ch, 1, ch), pstart.astype(F32)[:, None])
    dest = dest.reshape(-1)
    tok = jnp.repeat(jnp.arange(T, dtype=jnp.int32), TOP_K)
    tok_buf = jnp.zeros((n_rows,), jnp.int32).at[dest].set(tok)
    block_start = jnp.arange(n_blocks, dtype=jnp.int32) * MOE_BLOCK
    block_e = jnp.minimum(jnp.sum((pend[None, :] <= block_start[:, None]).astype(jnp.int32), axis=1),
                          N_EXPERTS - 1).astype(jnp.int32)
    n_used = (pend[-1] // MOE_BLOCK).astype(jnp.int32).reshape(1)
    return dest.reshape(T, TOP_K), tok_buf, block_e, n_used


def _moe_layer(x, g, shift, scale, gate, layer, w_router, b_router, w_gate, w_up, w_down,
               ws_gate, ws_up, ws_down):
    T, D = x.shape
    w_router_pad = jnp.zeros((D, LANES), F32).at[:, :N_EXPERTS].set(w_router)
    b_router_pad = jnp.zeros((1, LANES), F32).at[0, :N_EXPERTS].set(b_router)
    t_packed, top_e, top_w = _router(x, g, shift, scale, w_router_pad, b_router_pad)
    top_e, top_w = top_e[:, :TOP_K], top_w[:, :TOP_K]
    pos, tok_buf, block_e, n_used = _dispatch_plan(top_e)

    wg_t, wu_t = jnp.swapaxes(w_gate, -1, -2), jnp.swapaxes(w_up, -1, -2)
    wsg_t, wsu_t = jnp.swapaxes(ws_gate, -1, -2)[:, None], jnp.swapaxes(ws_up, -1, -2)[:, None]
    y = _experts(t_packed, tok_buf, wg_t, wu_t, w_down, layer, block_e, n_used)
    n_sh = T // MOE_BLOCK
    ysh = _experts(t_packed, jnp.arange(T, dtype=jnp.int32), wsg_t, wsu_t, ws_down[:, None], layer,
                   jnp.zeros((n_sh,), jnp.int32), jnp.full((1,), n_sh, jnp.int32))
    return _combine(x, ysh, y, pos, top_w, gate)


def _norm_mod_kernel(x_ref, g_ref, sh_ref, sc_ref, o_ref):
    _norm_modulate(x_ref, g_ref, sh_ref, sc_ref, o_ref)


def _norm_mod(x, g, shift, scale, tm=256):
    S, D = x.shape
    tm = _tile(S, tm)
    vec = pl.BlockSpec((1, D), lambda i: (0, 0))
    return pl.pallas_call(
        _norm_mod_kernel,
        grid=(S // tm,),
        in_specs=[pl.BlockSpec((tm, D), lambda i: (i, 0)), vec, vec, vec],
        out_specs=pl.BlockSpec((tm, D), lambda i: (i, 0)),
        out_shape=jax.ShapeDtypeStruct((S, D), BF16),
        compiler_params=_cparams("parallel"),
        name="norm_modulate",
    )(x, g, shift, scale)


def _inproj_t_kernel(wt_ref, hn_ref, b_ref, o_ref, *, groups):
    y = lax.dot_general(wt_ref[...], hn_ref[...], NT_DIMS, preferred_element_type=F32) + b_ref[...]
    per = y.shape[1] // FFT_N2
    i = pl.program_id(1)
    for s in range(groups):
        @pl.when(i % groups == s)
        def _():
            for q in range(per):
                o_ref[:, s * per + q, :] = y[:, q * FFT_N2:(q + 1) * FFT_N2]


def _inproj_t(hn, wt_bf16, bias_col, tm=512, tn=512):
    S, D = hn.shape
    N = wt_bf16.shape[0]
    tm, tn = _tile(S, tm), _tile(N, tn)
    n_chunks = S // FFT_N2
    per = tm // FFT_N2
    rows = min(8, n_chunks)
    groups = rows // per
    return pl.pallas_call(
        functools.partial(_inproj_t_kernel, groups=groups),
        grid=(N // tn, S // tm),
        in_specs=[
            pl.BlockSpec((tn, D), lambda j, i: (j, 0)),
            pl.BlockSpec((tm, D), lambda j, i: (i, 0)),
            pl.BlockSpec((tn, 1), lambda j, i: (j, 0)),
        ],
        out_specs=pl.BlockSpec((tn, rows, FFT_N2), lambda j, i: (j, i // groups, 0)),
        out_shape=jax.ShapeDtypeStruct((N, n_chunks, FFT_N2), F32),
        compiler_params=_cparams("parallel", "arbitrary"),
        name="hyena_inproj",
    )(wt_bf16, hn, bias_col)


def _filter_mlp_kernel(z_ref, w1_ref, b1_ref, w2_ref, b2_ref, w3_ref, b3_ref, fq_ref, o_ref):
    hp = lax.Precision.HIGHEST
    fq = fq_ref[...]
    h = jnp.sin(fq * (jnp.dot(z_ref[...], w1_ref[...], precision=hp,
                              preferred_element_type=F32) + b1_ref[...]))
    h = jnp.sin(fq * (jnp.dot(h, w2_ref[...], precision=hp,
                              preferred_element_type=F32) + b2_ref[...]))
    h = jnp.sin(fq * (jnp.dot(h, w3_ref[...], precision=hp,
                              preferred_element_type=F32) + b3_ref[...]))
    o_ref[...] = h.astype(o_ref.dtype)


def _filter_features(L, w1, b1, w2, b2, w3, b3, freq, tl=2048):
    Fh = w1.shape[1]
    t = jnp.linspace(0.0, 1.0, L, dtype=F32)[:, None]
    bands = (FILTER_EMB - 1) // 2
    f = jnp.linspace(1e-4, bands - 1, bands, dtype=F32)
    ang = (2.0 * math.pi / L) * jnp.arange(L, dtype=F32)[:, None] * f
    z = jnp.concatenate([t, jnp.cos(ang), -jnp.sin(ang)], axis=-1)
    zp = jnp.zeros((L, LANES), F32).at[:, :FILTER_EMB].set(z)
    w1p = jnp.zeros((LANES, Fh), F32).at[:FILTER_EMB].set(w1)
    tl = _tile(L, tl)
    row = lambda v: v.reshape(1, -1)
    const = lambda shape: pl.BlockSpec(shape, lambda i: (0, 0))
    feats = pl.pallas_call(
        _filter_mlp_kernel,
        grid=(L // tl,),
        in_specs=[pl.BlockSpec((tl, LANES), lambda i: (i, 0)),
                  const((LANES, Fh)), const((1, Fh)), const((Fh, Fh)), const((1, Fh)),
                  const((Fh, Fh)), const((1, Fh)), const((1, Fh))],
        out_specs=pl.BlockSpec((tl, Fh), lambda i: (i, 0)),
        out_shape=jax.ShapeDtypeStruct((L, Fh), BF16),
        compiler_params=_cparams("parallel"),
        name="hyena_filter_mlp",
    )(zp, w1p, row(b1), w2, row(b2), w3, row(b3), row(freq))
    zero = jnp.zeros((1, Fh), BF16)
    feats_all = jnp.concatenate([feats, zero, feats[:0:-1]], axis=0)
    t_all = jnp.concatenate([t[:, 0], t[:1, 0], t[:0:-1, 0]])[None, :]
    return feats_all, t_all


def _filter_taps_kernel(wt_ref, f_ref, t_ref, d_ref, o_ref):
    h = lax.dot_general(wt_ref[...], f_ref[...], NT_DIMS, preferred_element_type=F32)
    h = (h * jnp.exp(-(d_ref[...] * t_ref[...]))).astype(o_ref.dtype)
    for q in range(o_ref.shape[1]):
        o_ref[:, q, :] = h[:, q * FFT_N2:(q + 1) * FFT_N2]


def _filter_taps(wout_t, feats_all, t_all, deltas_col, tn=512):
    n_ord, _, D, Fh = wout_t.shape
    N = feats_all.shape[0]
    rows = 8
    tl = rows * FFT_N2
    assert (N // 2) % tl == 0, (N, tl)
    tn = _tile(D, tn)
    half = (N // 2) // tl
    return pl.pallas_call(
        _filter_taps_kernel,
        grid=(n_ord, D // tn, N // tl),
        in_specs=[
            pl.BlockSpec((None, None, tn, Fh), lambda n, c, l: (n, l // half, c, 0)),
            pl.BlockSpec((tl, Fh), lambda n, c, l: (l, 0)),
            pl.BlockSpec((1, tl), lambda n, c, l: (0, l)),
            pl.BlockSpec((tn, 1), lambda n, c, l: (c, 0)),
        ],
        out_specs=pl.BlockSpec((None, tn, rows, FFT_N2), lambda n, c, l: (n, c, l, 0)),
        out_shape=jax.ShapeDtypeStruct((n_ord, D, N // FFT_N2, FFT_N2), BF16),
        compiler_params=_cparams("parallel", "parallel", "arbitrary"),
        name="hyena_filter_taps",
    )(wout_t, feats_all, t_all, deltas_col)


def _dft_constants(N):
    N2 = FFT_N2
    N1 = N // N2
    n1 = np.arange(N1)
    th1 = 2.0 * np.pi * np.outer(n1, n1) / N1
    fl = np.concatenate([np.cos(th1), -np.sin(th1)], axis=0)
    tht = 2.0 * np.pi * np.outer(n1, np.arange(N2)) / N
    n2 = np.arange(N2)
    th2 = 2.0 * np.pi * np.outer(n2, n2) / N2
    fr, fi = np.cos(th2), -np.sin(th2)
    ilc = np.cos(th1) / N
    ils = -np.sin(th1) / N
    bf = lambda a: jnp.asarray(a, dtype=F32).astype(BF16)
    return dict(
        fl_full=bf(fl), fl_half=bf(fl[:, :N1 // 2]),
        tr=jnp.asarray(np.cos(tht), F32), ti=jnp.asarray(-np.sin(tht), F32),
        fr=bf(fr), fi=bf(fi), nfi=bf(-fi),
        ilc=bf(ilc[:N1 // 2]), ils=bf(ils[:N1 // 2]),
    )


def _dft_forward(z_of, n_ch, n1, fl_ref, tr_ref, ti_ref, fr_ref, fi_ref, nfi_ref, br_ref, bi_ref):
    tr, ti = tr_ref[...], ti_ref[...]
    for c in range(n_ch):
        a = jnp.dot(fl_ref[...], z_of(c), preferred_element_type=F32)
        ar, ai = a[:n1], a[n1:]
        br_ref[c * n1:(c + 1) * n1, :] = (ar * tr - ai * ti).astype(BF16)
        bi_ref[c * n1:(c + 1) * n1, :] = (ar * ti + ai * tr).astype(BF16)
    br, bi = br_ref[...], bi_ref[...]
    xr = (jnp.dot(br, fr_ref[...], preferred_element_type=F32)
          + jnp.dot(bi, nfi_ref[...], preferred_element_type=F32))
    xi = (jnp.dot(br, fi_ref[...], preferred_element_type=F32)
          + jnp.dot(bi, fr_ref[...], preferred_element_type=F32))
    return xr, xi


def _filter_fft_kernel(kk_ref, fl_ref, tr_ref, ti_ref, fr_ref, fi_ref, nfi_ref,
                       kr_ref, ki_ref, br_ref, bi_ref):
    n1 = tr_ref.shape[0]
    n_ch = kk_ref.shape[0] // n1

    def z_of(c):
        kk = kk_ref[c * n1:(c + 1) * n1, :]
        l1 = jnp.sum(jnp.abs(kk.astype(F32)), axis=-1, keepdims=True)
        l1 = jnp.sum(l1, axis=0, keepdims=True)
        return (kk.astype(F32) * (1.0 / l1)).astype(BF16)

    xr, xi = _dft_forward(z_of, n_ch, n1, fl_ref, tr_ref, ti_ref, fr_ref, fi_ref, nfi_ref,
                          br_ref, bi_ref)
    kr_ref[...] = xr.astype(kr_ref.dtype)
    ki_ref[...] = xi.astype(ki_ref.dtype)


def _filter_fft(kk, dc):
    n_ord, D, N1, _ = kk.shape
    cb = _tile(D, FFT_CB)
    rows = cb * N1
    kk2 = kk.reshape(n_ord, D * N1, FFT_N2)
    const = lambda a: pl.BlockSpec(a.shape, lambda n, c: (0, 0))
    consts = [dc['fl_full'], dc['tr'], dc['ti'], dc['fr'], dc['fi'], dc['nfi']]
    blk = pl.BlockSpec((None, rows, FFT_N2), lambda n, c: (n, c, 0))
    return pl.pallas_call(
        _filter_fft_kernel,
        grid=(n_ord, D // cb),
        in_specs=[blk] + [const(a) for a in consts],
        out_specs=[blk, blk],
        out_shape=[jax.ShapeDtypeStruct((n_ord, D * N1, FFT_N2), BF16)] * 2,
        scratch_shapes=[pltpu.VMEM((rows, FFT_N2), BF16)] * 2,
        compiler_params=_cparams("parallel", "parallel"),
        name="hyena_filter_fft",
    )(kk2, *consts)


def _short_conv(u, p, k, n1h):
    rows, n2 = u.shape
    lane = lax.broadcasted_iota(jnp.int32, u.shape, 1)
    r1 = lax.broadcasted_iota(jnp.int32, u.shape, 0) % n1h
    back = pltpu.roll(u, 1, axis=1)
    prev = jnp.where(lane == 0,
                     jnp.where(r1 == 0, 0.0, pltpu.roll(back, 1, axis=0)), back)
    fwd = pltpu.roll(u, n2 - 1, axis=1)
    nxt = jnp.where(lane == n2 - 1,
                    jnp.where(r1 == n1h - 1, 0.0, pltpu.roll(fwd, rows - 1, axis=0)), fwd)
    c = 4 * k
    return (p[:, c:c + 1] * prev + p[:, c + 1:c + 2] * u + p[:, c + 2:c + 3] * nxt
            + p[:, c + 3:c + 4])


def _hyena_conv_kernel(uv_ref, ug0_ref, ug1_ref, p_ref, kr_ref, ki_ref,
                       fl_ref, tr_ref, ti_ref, fr_ref, fi_ref, nfi_ref, ilc_ref, ils_ref,
                       o_ref, br_ref, bi_ref, y_ref):
    n1 = tr_ref.shape[0]
    n1h = n1 // 2
    n_ch = uv_ref.shape[0] // n1h
    p = p_ref[...]
    tr, ti = tr_ref[...], ti_ref[...]
    gate_refs = (ug0_ref, ug1_ref)

    z = _short_conv(uv_ref[...], p, 0, n1h)
    for n in range(HYENA_ORDER):
        zb = z.astype(BF16)
        xr, xi = _dft_forward(lambda c: zb[c * n1h:(c + 1) * n1h, :], n_ch, n1,
                              fl_ref, tr_ref, ti_ref, fr_ref, fi_ref, nfi_ref, br_ref, bi_ref)
        kr, ki = kr_ref[n].astype(F32), ki_ref[n].astype(F32)
        yr = (xr * kr - xi * ki).astype(BF16)
        yi = (xr * ki + xi * kr).astype(BF16)
        gr = (jnp.dot(yr, fr_ref[...], preferred_element_type=F32)
              + jnp.dot(yi, fi_ref[...], preferred_element_type=F32))
        gi = (jnp.dot(yi, fr_ref[...], preferred_element_type=F32)
              + jnp.dot(yr, nfi_ref[...], preferred_element_type=F32))
        for c in range(n_ch):
            grc, gic = gr[c * n1:(c + 1) * n1], gi[c * n1:(c + 1) * n1]
            hr = (grc * tr + gic * ti).astype(BF16)
            hi = (gic * tr - grc * ti).astype(BF16)
            y_ref[c * n1h:(c + 1) * n1h, :] = (
                jnp.dot(ilc_ref[...], hr, preferred_element_type=F32)
                + jnp.dot(ils_ref[...], hi, preferred_element_type=F32))
        gate = _short_conv(gate_refs[n][...], p, n + 1, n1h)
        z = gate * (y_ref[...] + p[:, 12 + n:13 + n] * z)
    o_ref[...] = z.astype(o_ref.dtype)


def _hyena_conv(ut, params_rows, kr, ki, dc, D, L):
    N1 = 2 * L // FFT_N2
    n1h = N1 // 2
    cb = _tile(D, FFT_CB)
    rows = cb * n1h
    u3 = ut.reshape(3, D * n1h, FFT_N2)
    ublk = lambda part: pl.BlockSpec((None, rows, FFT_N2), lambda c: (part, c, 0))
    const = lambda a: pl.BlockSpec(a.shape, lambda c: (0, 0))
    consts = [dc['fl_half'], dc['tr'], dc['ti'], dc['fr'], dc['fi'], dc['nfi'], dc['ilc'], dc['ils']]
    kblk = pl.BlockSpec((HYENA_ORDER, cb * N1, FFT_N2), lambda c: (0, c, 0))
    out = pl.pallas_call(
        _hyena_conv_kernel,
        grid=(D // cb,),
        in_specs=[ublk(0), ublk(1), ublk(2),
                  pl.BlockSpec((rows, 16), lambda c: (c, 0)), kblk, kblk]
                 + [const(a) for a in consts],
        out_specs=pl.BlockSpec((rows, FFT_N2), lambda c: (c, 0)),
        out_shape=jax.ShapeDtypeStruct((D * n1h, FFT_N2), BF16),
        scratch_shapes=[pltpu.VMEM((cb * N1, FFT_N2), BF16)] * 2
                       + [pltpu.VMEM((rows, FFT_N2), F32)],
        compiler_params=_cparams("parallel"),
        name="hyena_long_conv",
    )(u3, u3, u3, params_rows, kr, ki, *consts)
    return out.reshape(D, L)


def _hyena_layer(x, g, shift, scale, gate, w_in, b_in, conv_w, conv_b, f_w1, f_b1, f_w2, f_b2,
                 f_w3, f_b3, f_freq, f_wout, skip, w_out, b_out):
    L, D = x.shape
    n1h = L // FFT_N2
    hn = _norm_mod(x, g, shift, scale)
    ut = _inproj_t(hn, w_in.T.astype(BF16), b_in[:, None])

    feats_all, t_all = _filter_features(L, f_w1, f_b1, f_w2, f_b2, f_w3, f_b3, f_freq)
    deltas = jnp.abs(jnp.linspace(math.log(DECAY_TARGET) / SLOW_DECAY_PCT,
                                  math.log(DECAY_TARGET) / FAST_DECAY_PCT, D, dtype=F32))
    wout_t = jnp.transpose(f_wout, (1, 2, 3, 0)).astype(BF16)
    kk = _filter_taps(wout_t, feats_all, t_all, deltas[:, None])
    dc = _dft_constants(2 * L)
    kr, ki = _filter_fft(kk, dc)

    cw = conv_w.reshape(SHORT_CONV, HYENA_ORDER + 1, D)
    cb = conv_b.reshape(HYENA_ORDER + 1, D)
    cols = []
    for part in range(HYENA_ORDER + 1):
        cols += [cw[0, part], cw[1, part], cw[2, part], cb[part]]
    cols += [skip[0], skip[1], jnp.zeros((D,), F32), jnp.zeros((D,), F32)]
    params_rows = jnp.repeat(jnp.stack(cols, axis=1), n1h, axis=0)

    zt = _hyena_conv(ut, params_rows, kr, ki, dc, D, L)
    return _matmul(zt.T, w_out.astype(BF16), bias=b_out[None, :], resid=x, gate=gate,
                   name="hyena_outproj")


def _attention_layer(x, hc, g, shift, scale, gate, cshift, cscale, w_qkv, w_out, q_gain, k_gain, rpb):
    S, D = x.shape
    rows = S // GRID_W
    w_bf16 = w_qkv.astype(BF16)
    gains = jnp.stack([q_gain * (HEAD_DIM ** -0.5), k_gain, jnp.ones_like(k_gain)])[:, None, :]
    qkvh = _qkv_proj(x, g, shift, scale, w_bf16, gains)
    ctxh = _qkv_proj(hc, g, cshift, cscale, w_bf16, gains)
    bias = _attention_bias(rpb, rows)
    o = _attention(qkvh, ctxh, bias, S, D)
    return _matmul(o, w_out.astype(BF16), resid=x, gate=gate, name="attn_outproj")


def _ada_rows(sc, scc, ada_w, ada_b, layer):
    D = sc.shape[-1]
    a = jnp.zeros((8, D), F32).at[0].set(sc[0]).at[1].set(scc).astype(BF16)
    return _matmul(a, ada_w, bias=ada_b[layer][None, :], tm=8, tn=512, name="adaln", w_layer=layer)


def kernel(x, c, ctx, c_ctx, ada_w, ada_b, norm_mix_g, norm_ffn_g, na_w_qkv, na_w_out, na_q_gain, na_k_gain, na_rpb, hy_w_in, hy_b_in, hy_conv_w, hy_conv_b, hy_f_w1, hy_f_b1, hy_f_w2, hy_f_b2, hy_f_w3, hy_f_b3, hy_f_freq, hy_f_wout, hy_skip, hy_w_out, hy_b_out, moe_w_router, moe_b_router, moe_w_gate, moe_w_up, moe_w_down, moe_ws_gate, moe_ws_up, moe_ws_down):
    B, S, D = x.shape
    assert B == 1 and D % HEAD_DIM == 0 and S % (ATT_ROWS * GRID_W) == 0
    depth = ada_w.shape[0]
    sc = jax.nn.silu(c)
    scc = jax.nn.silu(c_ctx)
    xs = x[0]
    hc = ctx[0]
    row = lambda v: v.reshape(1, D)
    for i in range(depth):
        j = i // 2
        ada = _ada_rows(sc, scc, ada_w, ada_b, i)
        sh_a, sc_a, g_a, sh_f, sc_f, g_f = [row(v) for v in jnp.split(ada[0], 6)]
        if i % 2 == 0:
            csh_a, csc_a = [row(v) for v in jnp.split(ada[1], 6)[:2]]
            xs = _attention_layer(xs, hc, row(norm_mix_g[i]), sh_a, sc_a, g_a, csh_a, csc_a,
                                  na_w_qkv[j], na_w_out[j], na_q_gain[j], na_k_gain[j],
                                  na_rpb[j])
        else:
            xs = _hyena_layer(xs, row(norm_mix_g[i]), sh_a, sc_a, g_a,
                              hy_w_in[j], hy_b_in[j], hy_conv_w[j], hy_conv_b[j],
                              hy_f_w1[j], hy_f_b1[j], hy_f_w2[j], hy_f_b2[j], hy_f_w3[j], hy_f_b3[j],
                              hy_f_freq[j], hy_f_wout[j], hy_skip[j], hy_w_out[j], hy_b_out[j])
        xs = _moe_layer(xs, row(norm_ffn_g[i]), sh_f, sc_f, g_f, i,
                        moe_w_router[i], moe_b_router[i], moe_w_gate, moe_w_up, moe_w_down,
                        moe_ws_gate, moe_ws_up, moe_ws_down)
    return xs[None]
```

```python
import functools
import math

import numpy as np
import jax
import jax.numpy as jnp
from jax import lax
from jax.experimental import pallas as pl
from jax.experimental.pallas import tpu as pltpu

F32 = jnp.float32
BF16 = jnp.bfloat16

GRID_W = 64
HEAD_DIM = 128
WIN_ROWS = 8
WIN_COLS = 16
SHORT_CONV = 3
HYENA_ORDER = 2
FILTER_EMB = 33
DECAY_TARGET = 1e-2
FAST_DECAY_PCT = 0.3
SLOW_DECAY_PCT = 1.5
N_EXPERTS = 64
N_GROUPS = 8
TOPK_GROUPS = 4
TOP_K = 8
ROUTED_SCALE = 2.5
MOE_BLOCK = 256
NORM_EPS = 1e-6

LANES = 128
VMEM_LIMIT_BYTES = 48 << 20
MASK_VALUE = -1e30

ATT_ROWS = 8
ATT_HALO = 4
ATT_HEADS = 2
FFT_N2 = 256
FFT_CB = 8
NT_DIMS = (((1,), (1,)), ((), ()))


def _cparams(*sem):
    return pltpu.CompilerParams(dimension_semantics=sem, vmem_limit_bytes=VMEM_LIMIT_BYTES)


def _tile(n, t):
    t = min(n, t)
    assert n % t == 0, (n, t)
    return t


def _norm_modulate(x_ref, g_ref, sh_ref, sc_ref, out_ref, chunk=32):
    g = g_ref[...]
    sc1 = 1.0 + sc_ref[...]
    sh = sh_ref[...]
    rows = x_ref.shape[0]
    chunk = min(chunk, rows)

    def body(r, carry):
        sl = pl.ds(pl.multiple_of(r * chunk, chunk), chunk)
        x = x_ref[sl, :]
        ms = jnp.mean(x * x, axis=-1, keepdims=True)
        y = (x * lax.rsqrt(ms + NORM_EPS)) * g
        out_ref[sl, :] = (y * sc1 + sh).astype(out_ref.dtype)
        return carry

    lax.fori_loop(0, rows // chunk, body, 0)


def _qkv_kernel(x_ref, g_ref, sh_ref, sc_ref, w_ref, gain_ref, o_ref, hn_ref, *, n_norm_tiles):
    j = pl.program_id(1)

    @pl.when(j == 0)
    def _():
        _norm_modulate(x_ref, g_ref, sh_ref, sc_ref, hn_ref)

    y = jnp.dot(hn_ref[...], w_ref[...], preferred_element_type=F32)
    heads = [y[:, hh * HEAD_DIM:(hh + 1) * HEAD_DIM] for hh in range(o_ref.shape[0])]

    @pl.when(j < n_norm_tiles)
    def _():
        gain = gain_ref[...]
        for hh, yh in enumerate(heads):
            ms = jnp.mean(yh * yh, axis=-1, keepdims=True)
            o_ref[hh] = ((yh * lax.rsqrt(ms + NORM_EPS)) * gain).astype(o_ref.dtype)

    @pl.when(j >= n_norm_tiles)
    def _():
        for hh, yh in enumerate(heads):
            o_ref[hh] = yh.astype(o_ref.dtype)


def _qkv_proj(x, g, shift, scale, w_bf16, gains, tm=512, tn=512):
    S, D = x.shape
    N = w_bf16.shape[1]
    tm, tn = _tile(S, tm), _tile(D, tn)
    tiles_per_region = D // tn
    return pl.pallas_call(
        functools.partial(_qkv_kernel, n_norm_tiles=2 * tiles_per_region),
        grid=(S // tm, N // tn),
        in_specs=[
            pl.BlockSpec((tm, D), lambda i, j: (i, 0)),
            pl.BlockSpec((1, D), lambda i, j: (0, 0)),
            pl.BlockSpec((1, D), lambda i, j: (0, 0)),
            pl.BlockSpec((1, D), lambda i, j: (0, 0)),
            pl.BlockSpec((D, tn), lambda i, j: (0, j)),
            pl.BlockSpec((None, 1, HEAD_DIM), lambda i, j: (j // tiles_per_region, 0, 0)),
        ],
        out_specs=pl.BlockSpec((tn // HEAD_DIM, tm, HEAD_DIM), lambda i, j: (j, i, 0)),
        out_shape=jax.ShapeDtypeStruct((N // HEAD_DIM, S, HEAD_DIM), BF16),
        scratch_shapes=[pltpu.VMEM((tm, D), BF16)],
        compiler_params=_cparams("parallel", "arbitrary"),
        name="qkv_proj",
    )(x, g, shift, scale, w_bf16, gains)


def _attn_kernel(q_ref, kp_ref, kc_ref, kn_ref, vp_ref, vc_ref, vn_ref, kx_ref, vx_ref,
                 b_ref, o_ref):
    halo = ATT_HALO * GRID_W
    blk = ATT_ROWS * GRID_W
    bias_cols = (0, halo, halo + blk, 2 * halo + blk)
    for hh in range(q_ref.shape[0]):
        q = q_ref[hh]
        k_parts = (kp_ref[hh, blk - halo:, :], kc_ref[hh], kn_ref[hh, :halo, :], kx_ref[hh])
        v_parts = (vp_ref[hh, blk - halo:, :], vc_ref[hh], vn_ref[hh, :halo, :], vx_ref[hh])
        scores = []
        for idx, k in enumerate(k_parts):
            s = lax.dot_general(q, k, NT_DIMS, preferred_element_type=F32)
            if idx < 3:
                s = s + b_ref[hh, :, bias_cols[idx]:bias_cols[idx + 1]]
            scores.append(s)
        m = scores[0].max(axis=-1, keepdims=True)
        for s in scores[1:]:
            m = jnp.maximum(m, s.max(axis=-1, keepdims=True))
        den = jnp.zeros_like(m)
        acc = jnp.zeros((blk, HEAD_DIM), F32)
        for s, v in zip(scores, v_parts):
            p = jnp.exp(s - m)
            den = den + p.sum(axis=-1, keepdims=True)
            acc = acc + jnp.dot(p.astype(BF16), v, preferred_element_type=F32)
        o_ref[:, hh * HEAD_DIM:(hh + 1) * HEAD_DIM] = (acc / den).astype(o_ref.dtype)


def _attention_bias(rpb, rows):
    n_rb = rows // ATT_ROWS
    rb_rep = np.array([0, min(1, n_rb - 1), n_rb - 1])
    rl = np.arange(ATT_ROWS)
    i = np.arange(ATT_ROWS + 2 * ATT_HALO)
    r = rb_rep[:, None] * ATT_ROWS + rl[None, :]
    a = rb_rep[:, None] * ATT_ROWS - ATT_HALO + i[None, :]
    rs = np.clip(r - WIN_ROWS // 2, 0, rows - WIN_ROWS)
    row_ok = (a[:, None, :] >= rs[:, :, None]) & (a[:, None, :] < rs[:, :, None] + WIN_ROWS)
    drow = np.clip(a[:, None, :] - r[:, :, None] + (WIN_ROWS - 1), 0, 2 * WIN_ROWS - 2)
    col = np.arange(GRID_W)
    cs = np.clip(col - WIN_COLS // 2, 0, GRID_W - WIN_COLS)
    col_ok = (col[None, :] >= cs[:, None]) & (col[None, :] < cs[:, None] + WIN_COLS)
    dcol = np.clip(col[None, :] - col[:, None] + (WIN_COLS - 1), 0, 2 * WIN_COLS - 2)
    tcol = jnp.where(col_ok[None, None], rpb[:, :, dcol], MASK_VALUE)
    val = jnp.take(tcol, drow.reshape(-1), axis=1)
    H = rpb.shape[0]
    val = val.reshape((H,) + drow.shape + (GRID_W, GRID_W))
    val = jnp.where(row_ok[None, :, :, :, None, None], val, MASK_VALUE)
    val = jnp.transpose(val, (1, 0, 2, 4, 3, 5))
    return val.reshape(3, H, ATT_ROWS * GRID_W, (ATT_ROWS + 2 * ATT_HALO) * GRID_W).astype(F32)


def _attention(qkvh, ctxh, bias, S, D):
    H = D // HEAD_DIM
    C = ctxh.shape[1]
    blk = ATT_ROWS * GRID_W
    n_rb = S // blk
    nkeys = (ATT_ROWS + 2 * ATT_HALO) * GRID_W

    hp = _tile(H, ATT_HEADS)
    part = H // hp

    def spec(head_off, shift):
        return pl.BlockSpec(
            (hp, blk, HEAD_DIM),
            lambda h, rb: (head_off + h, jnp.clip(rb + shift, 0, n_rb - 1), 0))

    def variant(rb):
        return jnp.where(rb == 0, 0, jnp.where(rb == n_rb - 1, 2, 1))

    return pl.pallas_call(
        _attn_kernel,
        grid=(H // hp, n_rb),
        in_specs=[
            spec(0, 0),
            spec(part, -1), spec(part, 0), spec(part, 1),
            spec(2 * part, -1), spec(2 * part, 0), spec(2 * part, 1),
            pl.BlockSpec((hp, C, HEAD_DIM), lambda h, rb: (part + h, 0, 0)),
            pl.BlockSpec((hp, C, HEAD_DIM), lambda h, rb: (2 * part + h, 0, 0)),
            pl.BlockSpec((None, hp, blk, nkeys), lambda h, rb: (variant(rb), h, 0, 0)),
        ],
        out_specs=pl.BlockSpec((blk, hp * HEAD_DIM), lambda h, rb: (rb, h)),
        out_shape=jax.ShapeDtypeStruct((S, D), BF16),
        compiler_params=_cparams("parallel", "arbitrary"),
        name="nbr_attention",
    )(qkvh, qkvh, qkvh, qkvh, qkvh, qkvh, qkvh, ctxh, ctxh, bias)


def _mm_kernel(*refs, has_bias, has_resid):
    a_ref, w_ref = refs[0], refs[1]
    o_ref = refs[-1]
    y = jnp.dot(a_ref[...], w_ref[...].astype(BF16), preferred_element_type=F32)
    k = 2
    if has_bias:
        y = y + refs[k][...]
        k += 1
    if has_resid:
        y = refs[k][...] + refs[k + 1][...] * y
    o_ref[...] = y.astype(o_ref.dtype)


def _matmul(a, w, bias=None, resid=None, gate=None, out_dtype=F32, tm=512, tn=512, name="matmul",
            w_layer=None):
    M, K = a.shape
    N = w.shape[-1]
    tm, tn = _tile(M, tm), _tile(N, tn)
    if w_layer is None:
        w_spec = pl.BlockSpec((K, tn), lambda i, j: (0, j))
    else:
        w_spec = pl.BlockSpec((None, K, tn), lambda i, j: (w_layer, 0, j))
    in_specs = [pl.BlockSpec((tm, K), lambda i, j: (i, 0)), w_spec]
    args = [a, w]
    if bias is not None:
        in_specs.append(pl.BlockSpec((1, tn), lambda i, j: (0, j)))
        args.append(bias)
    if resid is not None:
        in_specs += [pl.BlockSpec((tm, tn), lambda i, j: (i, j)),
                     pl.BlockSpec((1, tn), lambda i, j: (0, j))]
        args += [resid, gate]
    return pl.pallas_call(
        functools.partial(_mm_kernel, has_bias=bias is not None, has_resid=resid is not None),
        grid=(M // tm, N // tn),
        in_specs=in_specs,
        out_specs=pl.BlockSpec((tm, tn), lambda i, j: (i, j)),
        out_shape=jax.ShapeDtypeStruct((M, N), out_dtype),
        compiler_params=_cparams("parallel", "arbitrary"),
        name=name,
    )(*args)


def _route_tile(logits, b_row):
    neg = -jnp.inf
    lane = lax.broadcasted_iota(jnp.int32, logits.shape, 1)
    lane_f = lane.astype(F32)
    grp = lane // (N_EXPERTS // N_GROUPS)
    scores = jax.nn.sigmoid(logits)
    choice = jnp.where(lane < N_EXPERTS, scores + b_row, neg)

    def first_max(c):
        m = jnp.max(c, axis=1, keepdims=True)
        idx = jnp.min(jnp.where(c == m, lane_f, float(LANES)), axis=1, keepdims=True)
        return m, idx

    grp_score = []
    for gi in range(N_GROUPS):
        cg = jnp.where(grp == gi, choice, neg)
        m1, i1 = first_max(cg)
        m2 = jnp.max(jnp.where(lane_f == i1, neg, cg), axis=1, keepdims=True)
        grp_score.append(m1 + m2)
    c = jnp.full(logits.shape, neg, F32)
    for gi in range(N_GROUPS):
        rank = jnp.zeros_like(grp_score[gi])
        for gj in range(N_GROUPS):
            if gj < gi:
                ahead = grp_score[gj] >= grp_score[gi]
            elif gj > gi:
                ahead = grp_score[gj] > grp_score[gi]
            else:
                continue
            rank = rank + jnp.where(ahead, 1.0, 0.0)
        c = jnp.where(grp == gi, jnp.where(rank < TOPK_GROUPS, choice, neg), c)

    ids = jnp.zeros(logits.shape, F32)
    wts = jnp.zeros(logits.shape, F32)
    total = jnp.zeros_like(grp_score[0])
    for k in range(TOP_K):
        _, idx = first_max(c)
        hit = lane_f == idx
        s = jnp.sum(jnp.where(hit, scores, 0.0), axis=1, keepdims=True)
        ids = jnp.where(lane == k, idx, ids)
        wts = jnp.where(lane == k, s, wts)
        total = total + s
        c = jnp.where(hit, neg, c)
    return ids, wts / total * ROUTED_SCALE


def _pack_halves(x):
    half = x.shape[1] // 2
    as_bits = lambda v: lax.bitcast_convert_type(v.astype(BF16).astype(F32), jnp.uint32)
    lo = lax.shift_right_logical(as_bits(x[:, :half]), jnp.uint32(16))
    hi = as_bits(x[:, half:]) & jnp.uint32(0xFFFF0000)
    return lo | hi


def _unpack_halves(u):
    lo = lax.bitcast_convert_type(lax.shift_left(u, jnp.uint32(16)), F32)
    hi = lax.bitcast_convert_type(u & jnp.uint32(0xFFFF0000), F32)
    return lo, hi


def _router_kernel(x_ref, g_ref, sh_ref, sc_ref, w_ref, b_ref, t_ref, e_ref, p_ref, hn_ref):
    _norm_modulate(x_ref, g_ref, sh_ref, sc_ref, hn_ref)
    hn = hn_ref[...]
    t_ref[...] = _pack_halves(hn)
    logits = jnp.dot(hn, w_ref[...], preferred_element_type=F32, precision=lax.Precision.HIGHEST)
    ids, wts = _route_tile(logits, b_ref[...])
    e_ref[...] = ids.astype(jnp.int32)
    p_ref[...] = wts


def _router(x, g, shift, scale, w_router_pad, b_router_pad, tm=256):
    T, D = x.shape
    NP = w_router_pad.shape[1]
    tm = _tile(T, tm)
    return pl.pallas_call(
        _router_kernel,
        grid=(T // tm,),
        in_specs=[
            pl.BlockSpec((tm, D), lambda i: (i, 0)),
            pl.BlockSpec((1, D), lambda i: (0, 0)),
            pl.BlockSpec((1, D), lambda i: (0, 0)),
            pl.BlockSpec((1, D), lambda i: (0, 0)),
            pl.BlockSpec((D, NP), lambda i: (0, 0)),
            pl.BlockSpec((1, NP), lambda i: (0, 0)),
        ],
        out_specs=[pl.BlockSpec((tm, D // 2), lambda i: (i, 0)),
                   pl.BlockSpec((tm, NP), lambda i: (i, 0)),
                   pl.BlockSpec((tm, NP), lambda i: (i, 0))],
        out_shape=[jax.ShapeDtypeStruct((T, D // 2), jnp.uint32),
                   jax.ShapeDtypeStruct((T, NP), jnp.int32),
                   jax.ShapeDtypeStruct((T, NP), F32)],
        scratch_shapes=[pltpu.VMEM((tm, D), F32)],
        compiler_params=_cparams("parallel"),
        name="moe_router",
    )(x, g, shift, scale, w_router_pad, b_router_pad)


def _row_gather_start(idx_ref, idx_base, n_rows, src_hbm, dst, sem):
    for r in range(n_rows):
        pltpu.make_async_copy(src_hbm.at[pl.ds(idx_ref[0, idx_base + r], 1), :],
                              dst.at[pl.ds(r, 1), :], sem).start(priority=r % 2)


def _gather_wait(dst, sem):
    pltpu.make_async_copy(dst, dst, sem).wait()


def _expert_kernel(be_ref, nused_ref, tok_ref, tok_next_ref, t_hbm, wg_ref, wu_ref, wd_ref,
                   y_ref, xbuf, sem, wbf):
    b = pl.program_id(0)
    n_used = nused_ref[0]
    slot = b % 2
    rows, half = xbuf.shape[1:]

    @pl.when((b == 0) | (be_ref[b] != be_ref[jnp.maximum(b - 1, 0)]))
    def _():
        wbf[0] = wg_ref[...].astype(BF16)
        wbf[1] = wu_ref[...].astype(BF16)
        wbf[2] = wd_ref[...].astype(BF16)

    @pl.when((b == 0) & (n_used > 0))
    def _():
        _row_gather_start(tok_ref, 0, rows, t_hbm, xbuf.at[0], sem.at[0])

    @pl.when(b + 1 < n_used)
    def _():
        _row_gather_start(tok_next_ref, 0, rows, t_hbm, xbuf.at[1 - slot], sem.at[1 - slot])

    @pl.when(b < n_used)
    def _():
        _gather_wait(xbuf.at[slot], sem.at[slot])
        x_lo, x_hi = _unpack_halves(xbuf[slot])
        x_lo, x_hi = x_lo.astype(BF16), x_hi.astype(BF16)
        gt = (lax.dot_general(x_lo, wbf[0, :, :half], NT_DIMS, preferred_element_type=F32)
              + lax.dot_general(x_hi, wbf[0, :, half:], NT_DIMS, preferred_element_type=F32))
        up = (lax.dot_general(x_lo, wbf[1, :, :half], NT_DIMS, preferred_element_type=F32)
              + lax.dot_general(x_hi, wbf[1, :, half:], NT_DIMS, preferred_element_type=F32))
        h = (gt * jax.nn.sigmoid(gt)) * up
        y = jnp.dot(h.astype(BF16), wbf[2], preferred_element_type=F32)
        y_ref[...] = _pack_halves(y)

    @pl.when(b >= n_used)
    def _():
        y_ref[...] = jnp.zeros(y_ref.shape, y_ref.dtype)


def _experts(t_packed, tok_buf, wg, wu, wd, layer, block_e, n_used):
    half = t_packed.shape[1]
    D = 2 * half
    Hd = wg.shape[-2]
    n_blocks = tok_buf.shape[0] // MOE_BLOCK
    tok3 = tok_buf.reshape(n_blocks, 1, MOE_BLOCK)
    once = pl.Buffered(1)
    smem_idx = lambda shift: pl.BlockSpec(
        (None, 1, MOE_BLOCK), lambda b, be, nu: (jnp.minimum(b + shift, n_blocks - 1), 0, 0),
        memory_space=pltpu.SMEM)
    wspec = lambda: pl.BlockSpec((None, None, Hd, D), lambda b, be, nu: (layer, be[b], 0, 0),
                                 pipeline_mode=once)
    return pl.pallas_call(
        _expert_kernel,
        grid_spec=pltpu.PrefetchScalarGridSpec(
            num_scalar_prefetch=2,
            grid=(n_blocks,),
            in_specs=[smem_idx(0), smem_idx(1), pl.BlockSpec(memory_space=pl.ANY),
                      wspec(), wspec(), wspec()],
            out_specs=pl.BlockSpec((MOE_BLOCK, half), lambda b, be, nu: (b, 0)),
            scratch_shapes=[pltpu.VMEM((2, MOE_BLOCK, half), jnp.uint32),
                            pltpu.SemaphoreType.DMA((2,)),
                            pltpu.VMEM((3, Hd, D), BF16)],
        ),
        out_shape=jax.ShapeDtypeStruct((n_blocks * MOE_BLOCK, half), jnp.uint32),
        compiler_params=_cparams("arbitrary"),
        name="moe_experts",
    )(block_e, n_used, tok3, tok3, t_packed, wg, wu, wd)


def _combine_kernel(pos_ref, pos_next_ref, x_ref, ysh_ref, w_ref, gate_ref, y_hbm, o_ref,
                    ybuf, sem):
    i = pl.program_id(0)
    slot = i % 2
    n_k, tm, half = ybuf.shape[1:]
    width = min(half, 2 * LANES)

    def start(idx_ref, s):
        for k in range(n_k):
            _row_gather_start(idx_ref, k * tm, tm, y_hbm, ybuf.at[s, k], sem.at[s])

    @pl.when(i == 0)
    def _():
        start(pos_ref, 0)

    @pl.when(i + 1 < pl.num_programs(0))
    def _():
        start(pos_next_ref, 1 - slot)

    _gather_wait(ybuf.at[slot], sem.at[slot])
    w = w_ref[...]
    wk = [w[:, k:k + 1] for k in range(n_k)]
    for s in range(half // width):
        c_lo, c_hi = pl.ds(s * width, width), pl.ds(half + s * width, width)
        acc_lo, acc_hi = _unpack_halves(ysh_ref[:, c_lo])
        for k in range(n_k):
            lo, hi = _unpack_halves(ybuf[slot, k, :, c_lo])
            acc_lo = acc_lo + wk[k] * lo
            acc_hi = acc_hi + wk[k] * hi
        o_ref[:, c_lo] = x_ref[:, c_lo] + gate_ref[:, c_lo] * acc_lo
        o_ref[:, c_hi] = x_ref[:, c_hi] + gate_ref[:, c_hi] * acc_hi


def _combine(x, ysh, y, pos, top_w, gate, tm=128):
    T, D = x.shape
    half = D // 2
    tm = _tile(T, tm)
    n_k = pos.shape[1]
    n_tiles = T // tm
    pos3 = jnp.transpose(pos.reshape(n_tiles, tm, n_k), (0, 2, 1)).reshape(n_tiles, 1, n_k * tm)
    smem_idx = lambda shift: pl.BlockSpec(
        (None, 1, n_k * tm), lambda i: (jnp.minimum(i + shift, n_tiles - 1), 0, 0),
        memory_space=pltpu.SMEM)
    return pl.pallas_call(
        _combine_kernel,
        grid=(n_tiles,),
        in_specs=[
            smem_idx(0), smem_idx(1),
            pl.BlockSpec((tm, D), lambda i: (i, 0)),
            pl.BlockSpec((tm, half), lambda i: (i, 0)),
            pl.BlockSpec((tm, n_k), lambda i: (i, 0)),
            pl.BlockSpec((1, D), lambda i: (0, 0)),
            pl.BlockSpec(memory_space=pl.ANY),
        ],
        out_specs=pl.BlockSpec((tm, D), lambda i: (i, 0)),
        out_shape=jax.ShapeDtypeStruct((T, D), F32),
        scratch_shapes=[pltpu.VMEM((2, n_k, tm, half), jnp.uint32),
                        pltpu.SemaphoreType.DMA((2,))],
        compiler_params=_cparams("arbitrary"),
        name="moe_combine",
    )(pos3, pos3, x, ysh, top_w, gate, y)


def _slot_kernel(eid_ref, pstart_ref, dest_ref, next_ref):
    n, _, ch = eid_ref.shape
    n_exp = pstart_ref.shape[0]
    next_ref[...] = pstart_ref[...]
    e_iota = lax.broadcasted_iota(jnp.int32, (n_exp, ch), 0)
    earlier = (lax.broadcasted_iota(jnp.int32, (ch, ch), 0)
               < lax.broadcasted_iota(jnp.int32, (ch, ch), 1)).astype(BF16)

    def body(r, carry):
        onehot = (eid_ref[r] == e_iota).astype(F32)
        before = jnp.dot(onehot.astype(BF16), earlier, preferred_element_type=F32)
        slot = jnp.sum(onehot * (next_ref[...] + before), axis=0, keepdims=True)
        dest_ref[r] = slot.astype(jnp.int32)
        next_ref[...] = next_ref[...] + jnp.sum(onehot, axis=1, keepdims=True)
        return carry

    lax.fori_loop(0, n, body, 0)


def _dispatch_plan(top_e, ch=256):
    T = top_e.shape[0]
    n_assign = T * TOP_K
    eid = top_e.reshape(-1).astype(jnp.int32)
    counts = jnp.sum((eid[:, None] == jnp.arange(N_EXPERTS)).astype(jnp.int32), axis=0)
    padded = (counts + MOE_BLOCK - 1) // MOE_BLOCK * MOE_BLOCK
    pend = jnp.cumsum(padded)
    pstart = pend - padded
    n_blocks = -(-n_assign // MOE_BLOCK) + N_EXPERTS
    n_rows = n_blocks * MOE_BLOCK
    ch = _tile(n_assign, ch)
    dest = pl.pallas_call(
        _slot_kernel,
        out_shape=jax.ShapeDtypeStruct((n_assign // ch, 1, ch), jnp.int32),
        scratch_shapes=[pltpu.VMEM((N_EXPERTS, 1), F32)],
        compiler_params=pltpu.CompilerParams(vmem_limit_bytes=VMEM_LIMIT_BYTES),
        name="moe_slots",
    )(eid.reshape(n_assign // ch, 1, ch), pstart.astype(F32)[:, None])
    dest = dest.reshape(-1)
    tok = jnp.repeat(jnp.arange(T, dtype=jnp.int32), TOP_K)
    tok_buf = jnp.zeros((n_rows,), jnp.int32).at[dest].set(tok)
    block_start = jnp.arange(n_blocks, dtype=jnp.int32) * MOE_BLOCK
    block_e = jnp.minimum(jnp.sum((pend[None, :] <= block_start[:, None]).astype(jnp.int32), axis=1),
                          N_EXPERTS - 1).astype(jnp.int32)
    n_used = (pend[-1] // MOE_BLOCK).astype(jnp.int32).reshape(1)
    return dest.reshape(T, TOP_K), tok_buf, block_e, n_used


def _moe_layer(x, g, shift, scale, gate, layer, w_router, b_router, w_gate, w_up, w_down,
               ws_gate, ws_up, ws_down):
    T, D = x.shape
    w_router_pad = jnp.zeros((D, LANES), F32).at[:, :N_EXPERTS].set(w_router)
    b_router_pad = jnp.zeros((1, LANES), F32).at[0, :N_EXPERTS].set(b_router)
    t_packed, top_e, top_w = _router(x, g, shift, scale, w_router_pad, b_router_pad)
    top_e, top_w = top_e[:, :TOP_K], top_w[:, :TOP_K]
    pos, tok_buf, block_e, n_used = _dispatch_plan(top_e)

    wg_t, wu_t = jnp.swapaxes(w_gate, -1, -2), jnp.swapaxes(w_up, -1, -2)
    wsg_t, wsu_t = jnp.swapaxes(ws_gate, -1, -2)[:, None], jnp.swapaxes(ws_up, -1, -2)[:, None]
    y = _experts(t_packed, tok_buf, wg_t, wu_t, w_down, layer, block_e, n_used)
    n_sh = T // MOE_BLOCK
    ysh = _experts(t_packed, jnp.arange(T, dtype=jnp.int32), wsg_t, wsu_t, ws_down[:, None], layer,
                   jnp.zeros((n_sh,), jnp.int32), jnp.full((1,), n_sh, jnp.int32))
    return _combine(x, ysh, y, pos, top_w, gate)


def _norm_mod_kernel(x_ref, g_ref, sh_ref, sc_ref, o_ref):
    _norm_modulate(x_ref, g_ref, sh_ref, sc_ref, o_ref)


def _norm_mod(x, g, shift, scale, tm=256):
    S, D = x.shape
    tm = _tile(S, tm)
    vec = pl.BlockSpec((1, D), lambda i: (0, 0))
    return pl.pallas_call(
        _norm_mod_kernel,
        grid=(S // tm,),
        in_specs=[pl.BlockSpec((tm, D), lambda i: (i, 0)), vec, vec, vec],
        out_specs=pl.BlockSpec((tm, D), lambda i: (i, 0)),
        out_shape=jax.ShapeDtypeStruct((S, D), BF16),
        compiler_params=_cparams("parallel"),
        name="norm_modulate",
    )(x, g, shift, scale)


def _inproj_t_kernel(wt_ref, hn_ref, b_ref, o_ref, *, groups):
    y = lax.dot_general(wt_ref[...], hn_ref[...], NT_DIMS, preferred_element_type=F32) + b_ref[...]
    per = y.shape[1] // FFT_N2
    i = pl.program_id(1)
    for s in range(groups):
        @pl.when(i % groups == s)
        def _():
            for q in range(per):
                o_ref[:, s * per + q, :] = y[:, q * FFT_N2:(q + 1) * FFT_N2]


def _inproj_t(hn, wt_bf16, bias_col, tm=512, tn=512):
    S, D = hn.shape
    N = wt_bf16.shape[0]
    tm, tn = _tile(S, tm), _tile(N, tn)
    n_chunks = S // FFT_N2
    per = tm // FFT_N2
    rows = min(8, n_chunks)
    groups = rows // per
    return pl.pallas_call(
        functools.partial(_inproj_t_kernel, groups=groups),
        grid=(N // tn, S // tm),
        in_specs=[
            pl.BlockSpec((tn, D), lambda j, i: (j, 0)),
            pl.BlockSpec((tm, D), lambda j, i: (i, 0)),
            pl.BlockSpec((tn, 1), lambda j, i: (j, 0)),
        ],
        out_specs=pl.BlockSpec((tn, rows, FFT_N2), lambda j, i: (j, i // groups, 0)),
        out_shape=jax.ShapeDtypeStruct((N, n_chunks, FFT_N2), F32),
        compiler_params=_cparams("parallel", "arbitrary"),
        name="hyena_inproj",
    )(wt_bf16, hn, bias_col)


def _filter_mlp_kernel(z_ref, w1_ref, b1_ref, w2_ref, b2_ref, w3_ref, b3_ref, fq_ref, o_ref):
    hp = lax.Precision.HIGHEST
    fq = fq_ref[...]
    h = jnp.sin(fq * (jnp.dot(z_ref[...], w1_ref[...], precision=hp,
                              preferred_element_type=F32) + b1_ref[...]))
    h = jnp.sin(fq * (jnp.dot(h, w2_ref[...], precision=hp,
                              preferred_element_type=F32) + b2_ref[...]))
    h = jnp.sin(fq * (jnp.dot(h, w3_ref[...], precision=hp,
                              preferred_element_type=F32) + b3_ref[...]))
    o_ref[...] = h.astype(o_ref.dtype)


def _filter_features(L, w1, b1, w2, b2, w3, b3, freq, tl=2048):
    Fh = w1.shape[1]
    t = jnp.linspace(0.0, 1.0, L, dtype=F32)[:, None]
    bands = (FILTER_EMB - 1) // 2
    f = jnp.linspace(1e-4, bands - 1, bands, dtype=F32)
    ang = (2.0 * math.pi / L) * jnp.arange(L, dtype=F32)[:, None] * f
    z = jnp.concatenate([t, jnp.cos(ang), -jnp.sin(ang)], axis=-1)
    zp = jnp.zeros((L, LANES), F32).at[:, :FILTER_EMB].set(z)
    w1p = jnp.zeros((LANES, Fh), F32).at[:FILTER_EMB].set(w1)
    tl = _tile(L, tl)
    row = lambda v: v.reshape(1, -1)
    const = lambda shape: pl.BlockSpec(shape, lambda i: (0, 0))
    feats = pl.pallas_call(
        _filter_mlp_kernel,
        grid=(L // tl,),
        in_specs=[pl.BlockSpec((tl, LANES), lambda i: (i, 0)),
                  const((LANES, Fh)), const((1, Fh)), const((Fh, Fh)), const((1, Fh)),
                  const((Fh, Fh)), const((1, Fh)), const((1, Fh))],
        out_specs=pl.BlockSpec((tl, Fh), lambda i: (i, 0)),
        out_shape=jax.ShapeDtypeStruct((L, Fh), BF16),
        compiler_params=_cparams("parallel"),
        name="hyena_filter_mlp",
    )(zp, w1p, row(b1), w2, row(b2), w3, row(b3), row(freq))
    zero = jnp.zeros((1, Fh), BF16)
    feats_all = jnp.concatenate([feats, zero, feats[:0:-1]], axis=0)
    t_all = jnp.concatenate([t[:, 0], t[:1, 0], t[:0:-1, 0]])[None, :]
    return feats_all, t_all


def _filter_taps_kernel(wt_ref, f_ref, t_ref, d_ref, o_ref):
    h = lax.dot_general(wt_ref[...], f_ref[...], NT_DIMS, preferred_element_type=F32)
    h = (h * jnp.exp(-(d_ref[...] * t_ref[...]))).astype(o_ref.dtype)
    for q in range(o_ref.shape[1]):
        o_ref[:, q, :] = h[:, q * FFT_N2:(q + 1) * FFT_N2]


def _filter_taps(wout_t, feats_all, t_all, deltas_col, tn=512):
    n_ord, _, D, Fh = wout_t.shape
    N = feats_all.shape[0]
    rows = 8
    tl = rows * FFT_N2
    assert (N // 2) % tl == 0, (N, tl)
    tn = _tile(D, tn)
    half = (N // 2) // tl
    return pl.pallas_call(
        _filter_taps_kernel,
        grid=(n_ord, D // tn, N // tl),
        in_specs=[
            pl.BlockSpec((None, None, tn, Fh), lambda n, c, l: (n, l // half, c, 0)),
            pl.BlockSpec((tl, Fh), lambda n, c, l: (l, 0)),
            pl.BlockSpec((1, tl), lambda n, c, l: (0, l)),
            pl.BlockSpec((tn, 1), lambda n, c, l: (c, 0)),
        ],
        out_specs=pl.BlockSpec((None, tn, rows, FFT_N2), lambda n, c, l: (n, c, l, 0)),
        out_shape=jax.ShapeDtypeStruct((n_ord, D, N // FFT_N2, FFT_N2), BF16),
        compiler_params=_cparams("parallel", "parallel", "arbitrary"),
        name="hyena_filter_taps",
    )(wout_t, feats_all, t_all, deltas_col)


def _dft_constants(N):
    N2 = FFT_N2
    N1 = N // N2
    n1 = np.arange(N1)
    th1 = 2.0 * np.pi * np.outer(n1, n1) / N1
    fl = np.concatenate([np.cos(th1), -np.sin(th1)], axis=0)
    tht = 2.0 * np.pi * np.outer(n1, np.arange(N2)) / N
    n2 = np.arange(N2)
    th2 = 2.0 * np.pi * np.outer(n2, n2) / N2
    fr, fi = np.cos(th2), -np.sin(th2)
    ilc = np.cos(th1) / N
    ils = -np.sin(th1) / N
    bf = lambda a: jnp.asarray(a, dtype=F32).astype(BF16)
    return dict(
        fl_full=bf(fl), fl_half=bf(fl[:, :N1 // 2]),
        tr=jnp.asarray(np.cos(tht), F32), ti=jnp.asarray(-np.sin(tht), F32),
        fr=bf(fr), fi=bf(fi), nfi=bf(-fi),
        ilc=bf(ilc[:N1 // 2]), ils=bf(ils[:N1 // 2]),
    )


def _dft_forward(z_of, n_ch, n1, fl_ref, tr_ref, ti_ref, fr_ref, fi_ref, nfi_ref, br_ref, bi_ref):
    tr, ti = tr_ref[...], ti_ref[...]
    for c in range(n_ch):
        a = jnp.dot(fl_ref[...], z_of(c), preferred_element_type=F32)
        ar, ai = a[:n1], a[n1:]
        br_ref[c * n1:(c + 1) * n1, :] = (ar * tr - ai * ti).astype(BF16)
        bi_ref[c * n1:(c + 1) * n1, :] = (ar * ti + ai * tr).astype(BF16)
    br, bi = br_ref[...], bi_ref[...]
    xr = (jnp.dot(br, fr_ref[...], preferred_element_type=F32)
          + jnp.dot(bi, nfi_ref[...], preferred_element_type=F32))
    xi = (jnp.dot(br, fi_ref[...], preferred_element_type=F32)
          + jnp.dot(bi, fr_ref[...], preferred_element_type=F32))
    return xr, xi


def _filter_fft_kernel(kk_ref, fl_ref, tr_ref, ti_ref, fr_ref, fi_ref, nfi_ref,
                       kr_ref, ki_ref, br_ref, bi_ref):
    n1 = tr_ref.shape[0]
    n_ch = kk_ref.shape[0] // n1

    def z_of(c):
        kk = kk_ref[c * n1:(c + 1) * n1, :]
        l1 = jnp.sum(jnp.abs(kk.astype(F32)), axis=-1, keepdims=True)
        l1 = jnp.sum(l1, axis=0, keepdims=True)
        return (kk.astype(F32) * (1.0 / l1)).astype(BF16)

    xr, xi = _dft_forward(z_of, n_ch, n1, fl_ref, tr_ref, ti_ref, fr_ref, fi_ref, nfi_ref,
                          br_ref, bi_ref)
    kr_ref[...] = xr.astype(kr_ref.dtype)
    ki_ref[...] = xi.astype(ki_ref.dtype)


def _filter_fft(kk, dc):
    n_ord, D, N1, _ = kk.shape
    cb = _tile(D, FFT_CB)
    rows = cb * N1
    kk2 = kk.reshape(n_ord, D * N1, FFT_N2)
    const = lambda a: pl.BlockSpec(a.shape, lambda n, c: (0, 0))
    consts = [dc['fl_full'], dc['tr'], dc['ti'], dc['fr'], dc['fi'], dc['nfi']]
    blk = pl.BlockSpec((None, rows, FFT_N2), lambda n, c: (n, c, 0))
    return pl.pallas_call(
        _filter_fft_kernel,
        grid=(n_ord, D // cb),
        in_specs=[blk] + [const(a) for a in consts],
        out_specs=[blk, blk],
        out_shape=[jax.ShapeDtypeStruct((n_ord, D * N1, FFT_N2), BF16)] * 2,
        scratch_shapes=[pltpu.VMEM((rows, FFT_N2), BF16)] * 2,
        compiler_params=_cparams("parallel", "parallel"),
        name="hyena_filter_fft",
    )(kk2, *consts)


def _short_conv(u, p, k, n1h):
    rows, n2 = u.shape
    lane = lax.broadcasted_iota(jnp.int32, u.shape, 1)
    r1 = lax.broadcasted_iota(jnp.int32, u.shape, 0) % n1h
    back = pltpu.roll(u, 1, axis=1)
    prev = jnp.where(lane == 0,
                     jnp.where(r1 == 0, 0.0, pltpu.roll(back, 1, axis=0)), back)
    fwd = pltpu.roll(u, n2 - 1, axis=1)
    nxt = jnp.where(lane == n2 - 1,
                    jnp.where(r1 == n1h - 1, 0.0, pltpu.roll(fwd, rows - 1, axis=0)), fwd)
    c = 4 * k
    return (p[:, c:c + 1] * prev + p[:, c + 1:c + 2] * u + p[:, c + 2:c + 3] * nxt
            + p[:, c + 3:c + 4])


def _hyena_conv_kernel(uv_ref, ug0_ref, ug1_ref, p_ref, kr_ref, ki_ref,
                       fl_ref, tr_ref, ti_ref, fr_ref, fi_ref, nfi_ref, ilc_ref, ils_ref,
                       o_ref, br_ref, bi_ref, y_ref):
    n1 = tr_ref.shape[0]
    n1h = n1 // 2
    n_ch = uv_ref.shape[0] // n1h
    p = p_ref[...]
    tr, ti = tr_ref[...], ti_ref[...]
    gate_refs = (ug0_ref, ug1_ref)

    z = _short_conv(uv_ref[...], p, 0, n1h)
    for n in range(HYENA_ORDER):
        zb = z.astype(BF16)
        xr, xi = _dft_forward(lambda c: zb[c * n1h:(c + 1) * n1h, :], n_ch, n1,
                              fl_ref, tr_ref, ti_ref, fr_ref, fi_ref, nfi_ref, br_ref, bi_ref)
        kr, ki = kr_ref[n].astype(F32), ki_ref[n].astype(F32)
        yr = (xr * kr - xi * ki).astype(BF16)
        yi = (xr * ki + xi * kr).astype(BF16)
        gr = (jnp.dot(yr, fr_ref[...], preferred_element_type=F32)
              + jnp.dot(yi, fi_ref[...], preferred_element_type=F32))
        gi = (jnp.dot(yi, fr_ref[...], preferred_element_type=F32)
              + jnp.dot(yr, nfi_ref[...], preferred_element_type=F32))
        for c in range(n_ch):
            grc, gic = gr[c * n1:(c + 1) * n1], gi[c * n1:(c + 1) * n1]
            hr = (grc * tr + gic * ti).astype(BF16)
            hi = (gic * tr - grc * ti).astype(BF16)
            y_ref[c * n1h:(c + 1) * n1h, :] = (
                jnp.dot(ilc_ref[...], hr, preferred_element_type=F32)
                + jnp.dot(ils_ref[...], hi, preferred_element_type=F32))
        gate = _short_conv(gate_refs[n][...], p, n + 1, n1h)
        z = gate * (y_ref[...] + p[:, 12 + n:13 + n] * z)
    o_ref[...] = z.astype(o_ref.dtype)


def _hyena_conv(ut, params_rows, kr, ki, dc, D, L):
    N1 = 2 * L // FFT_N2
    n1h = N1 // 2
    cb = _tile(D, FFT_CB)
    rows = cb * n1h
    u3 = ut.reshape(3, D * n1h, FFT_N2)
    ublk = lambda part: pl.BlockSpec((None, rows, FFT_N2), lambda c: (part, c, 0))
    const = lambda a: pl.BlockSpec(a.shape, lambda c: (0, 0))
    consts = [dc['fl_half'], dc['tr'], dc['ti'], dc['fr'], dc['fi'], dc['nfi'], dc['ilc'], dc['ils']]
    kblk = pl.BlockSpec((HYENA_ORDER, cb * N1, FFT_N2), lambda c: (0, c, 0))
    out = pl.pallas_call(
        _hyena_conv_kernel,
        grid=(D // cb,),
        in_specs=[ublk(0), ublk(1), ublk(2),
                  pl.BlockSpec((rows, 16), lambda c: (c, 0)), kblk, kblk]
                 + [const(a) for a in consts],
        out_specs=pl.BlockSpec((rows, FFT_N2), lambda c: (c, 0)),
        out_shape=jax.ShapeDtypeStruct((D * n1h, FFT_N2), BF16),
        scratch_shapes=[pltpu.VMEM((cb * N1, FFT_N2), BF16)] * 2
                       + [pltpu.VMEM((rows, FFT_N2), F32)],
        compiler_params=_cparams("parallel"),
        name="hyena_long_conv",
    )(u3, u3, u3, params_rows, kr, ki, *consts)
    return out.reshape(D, L)


def _hyena_layer(x, g, shift, scale, gate, w_in, b_in, conv_w, conv_b, f_w1, f_b1, f_w2, f_b2,
                 f_w3, f_b3, f_freq, f_wout, skip, w_out, b_out):
    L, D = x.shape
    n1h = L // FFT_N2
    hn = _norm_mod(x, g, shift, scale)
    ut = _inproj_t(hn, w_in.T.astype(BF16), b_in[:, None])

    feats_all, t_all = _filter_features(L, f_w1, f_b1, f_w2, f_b2, f_w3, f_b3, f_freq)
    deltas = jnp.abs(jnp.linspace(math.log(DECAY_TARGET) / SLOW_DECAY_PCT,
                                  math.log(DECAY_TARGET) / FAST_DECAY_PCT, D, dtype=F32))
    wout_t = jnp.transpose(f_wout, (1, 2, 3, 0)).astype(BF16)
    kk = _filter_taps(wout_t, feats_all, t_all, deltas[:, None])
    dc = _dft_constants(2 * L)
    kr, ki = _filter_fft(kk, dc)

    cw = conv_w.reshape(SHORT_CONV, HYENA_ORDER + 1, D)
    cb = conv_b.reshape(HYENA_ORDER + 1, D)
    cols = []
    for part in range(HYENA_ORDER + 1):
        cols += [cw[0, part], cw[1, part], cw[2, part], cb[part]]
    cols += [skip[0], skip[1], jnp.zeros((D,), F32), jnp.zeros((D,), F32)]
    params_rows = jnp.repeat(jnp.stack(cols, axis=1), n1h, axis=0)

    zt = _hyena_conv(ut, params_rows, kr, ki, dc, D, L)
    return _matmul(zt.T, w_out.astype(BF16), bias=b_out[None, :], resid=x, gate=gate,
                   name="hyena_outproj")


def _attention_layer(x, hc, g, shift, scale, gate, cshift, cscale, w_qkv, w_out, q_gain, k_gain, rpb):
    S, D = x.shape
    rows = S // GRID_W
    w_bf16 = w_qkv.astype(BF16)
    gains = jnp.stack([q_gain * (HEAD_DIM ** -0.5), k_gain, jnp.ones_like(k_gain)])[:, None, :]
    qkvh = _qkv_proj(x, g, shift, scale, w_bf16, gains)
    ctxh = _qkv_proj(hc, g, cshift, cscale, w_bf16, gains)
    bias = _attention_bias(rpb, rows)
    o = _attention(qkvh, ctxh, bias, S, D)
    return _matmul(o, w_out.astype(BF16), resid=x, gate=gate, name="attn_outproj")


def _ada_rows(sc, scc, ada_w, ada_b, layer):
    D = sc.shape[-1]
    a = jnp.zeros((8, D), F32).at[0].set(sc[0]).at[1].set(scc).astype(BF16)
    return _matmul(a, ada_w, bias=ada_b[layer][None, :], tm=8, tn=512, name="adaln", w_layer=layer)


def kernel(x, c, ctx, c_ctx, ada_w, ada_b, norm_mix_g, norm_ffn_g, na_w_qkv, na_w_out, na_q_gain, na_k_gain, na_rpb, hy_w_in, hy_b_in, hy_conv_w, hy_conv_b, hy_f_w1, hy_f_b1, hy_f_w2, hy_f_b2, hy_f_w3, hy_f_b3, hy_f_freq, hy_f_wout, hy_skip, hy_w_out, hy_b_out, moe_w_router, moe_b_router, moe_w_gate, moe_w_up, moe_w_down, moe_ws_gate, moe_ws_up, moe_ws_down):
    B, S, D = x.shape
    assert B == 1 and D % HEAD_DIM == 0 and S % (ATT_ROWS * GRID_W) == 0
    depth = ada_w.shape[0]
    sc = jax.nn.silu(c)
    scc = jax.nn.silu(c_ctx)
    xs = x[0]
    hc = ctx[0]
    row = lambda v: v.reshape(1, D)
    for i in range(depth):
        j = i // 2
        ada = _ada_rows(sc, scc, ada_w, ada_b, i)
        sh_a, sc_a, g_a, sh_f, sc_f, g_f = [row(v) for v in jnp.split(ada[0], 6)]
        if i % 2 == 0:
            csh_a, csc_a = [row(v) for v in jnp.split(ada[1], 6)[:2]]
            xs = _attention_layer(xs, hc, row(norm_mix_g[i]), sh_a, sc_a, g_a, csh_a, csc_a,
                                  na_w_qkv[j], na_w_out[j], na_q_gain[j], na_k_gain[j],
                                  na_rpb[j])
        else:
            xs = _hyena_layer(xs, row(norm_mix_g[i]), sh_a, sc_a, g_a,
                              hy_w_in[j], hy_b_in[j], hy_conv_w[j], hy_conv_b[j],
                              hy_f_w1[j], hy_f_b1[j], hy_f_w2[j], hy_f_b2[j], hy_f_w3[j], hy_f_b3[j],
                              hy_f_freq[j], hy_f_wout[j], hy_skip[j], hy_w_out[j], hy_b_out[j])
        xs = _moe_layer(xs, row(norm_ffn_g[i]), sh_f, sc_f, g_f, i,
                        moe_w_router[i], moe_b_router[i], moe_w_gate, moe_w_up, moe_w_down,
                        moe_ws_gate, moe_ws_up, moe_ws_down)
    return xs[None]
```
